```python
import math
import jax, jax.numpy as jnp
from jax import lax
import numpy as np

D_MODEL = 2048
BATCH = 8
SEQ = 4096
DEPTH = 4

D_MIX = D_MODEL
FOURIER_GROUPS = 4
FOURIER_GROUP_DIM = D_MODEL // 16
D_FOURIER = FOURIER_GROUPS * FOURIER_GROUP_DIM
D_HGRN = D_MIX - D_FOURIER
HG_DIM = 128
HG_HEADS = D_HGRN // HG_DIM
D_IN = D_FOURIER + 5 * D_HGRN
CHUNK = 64
D_FF = ((8 * D_MODEL // 3 + 127) // 128) * 128
ALPHA = (2 * DEPTH) ** 0.25
BETA = (8 * DEPTH) ** -0.25
N_MOD = 9
ADA_INIT = 0.2
LN_EPS = 1e-5
RMS_EPS = 1e-6

kernel_name = "fourier_hgrn2_macaron_deepnorm_adaln"


def layer_norm(x, g, b):
    xf = x.astype(jnp.float32)
    mu = jnp.mean(xf, axis=-1, keepdims=True)
    var = jnp.mean(jnp.square(xf - mu), axis=-1, keepdims=True)
    y = (xf - mu) * lax.rsqrt(var + LN_EPS) * g.astype(jnp.float32) + b.astype(jnp.float32)
    return y.astype(x.dtype)


def rms_norm_heads(y, gain, n_heads):
    B, S, W = y.shape
    yf = y.astype(jnp.float32).reshape(B, S, n_heads, W // n_heads)
    yf = yf * lax.rsqrt(jnp.mean(jnp.square(yf), axis=-1, keepdims=True) + RMS_EPS)
    return yf.reshape(B, S, W) * gain.astype(jnp.float32)


def modulate(x, shift, scale):
    return x * (1.0 + scale[:, None, :]) + shift[:, None, :]


def swiglu(h, w_in, w_out):
    a, b = jnp.split(h @ w_in, 2, axis=-1)
    return (jax.nn.silu(a) * b) @ w_out


def fourier_mix(u):
    B, S, _ = u.shape
    uf = u.astype(jnp.float32).reshape(B, S, FOURIER_GROUPS, FOURIER_GROUP_DIM)
    y = jnp.fft.fft2(uf, axes=(1, 3), norm="ortho").real
    return y.reshape(B, S, D_FOURIER)


def gla_chunked(q, k, v, log_f):
    B, S, H, K = q.shape
    V = v.shape[-1]
    N = S // CHUNK

    def blocks(t):
        return t.reshape(B, N, CHUNK, H, t.shape[-1]).transpose(1, 0, 3, 2, 4)

    q, k, v, log_f = blocks(q), blocks(k), blocks(v), blocks(log_f)
    b = jnp.cumsum(log_f, axis=3)
    b_ref = b[:, :, :, CHUNK // 2:CHUNK // 2 + 1]
    q_in = q * jnp.exp(b - b_ref)
    k_in = k * jnp.exp(b_ref - b)
    scores = jnp.einsum('nbhtk,nbhsk->nbhts', q_in, k_in)
    mask = jnp.tril(jnp.ones((CHUNK, CHUNK), dtype=bool))
    o_intra = jnp.einsum('nbhts,nbhsv->nbhtv', jnp.where(mask, scores, 0.0), v)
    b_last = b[:, :, :, -1:]
    q_st = q * jnp.exp(b)
    k_st = k * jnp.exp(b_last - b)
    decay = jnp.exp(b_last[:, :, :, 0])

    def step(state, xs):
        q_c, k_c, v_c, d_c = xs
        o_c = jnp.einsum('bhtk,bhkv->bhtv', q_c, state)
        state = d_c[..., None] * state + jnp.einsum('bhsk,bhsv->bhkv', k_c, v_c)
        return state, o_c

    state0 = jnp.zeros((B, H, K, V), jnp.float32)
    _, o_inter = lax.scan(step, state0, (q_st, k_st, v, decay))
    o = o_intra + o_inter
    return o.transpose(1, 0, 3, 2, 4).reshape(B, S, H, V)


def hgrn2_direction(q, v, z_f, lb):
    f = lb + (1.0 - lb) * jax.nn.sigmoid(z_f)
    return gla_chunked(q, 1.0 - f, v, jnp.log(f))


def bidirectional_hgrn2(q_raw, v_raw, zf_fwd, zf_bwd, lb_fwd, lb_bwd):
    B, S, _ = q_raw.shape

    def heads(t):
        return t.astype(jnp.float32).reshape(B, S, HG_HEADS, HG_DIM)

    def flip(t):
        return jnp.flip(t, axis=1)

    q = jax.nn.silu(heads(q_raw))
    v = heads(v_raw)
    lb_f = lb_fwd.reshape(HG_HEADS, HG_DIM)
    lb_b = lb_bwd.reshape(HG_HEADS, HG_DIM)
    o_fwd = hgrn2_direction(q, v, heads(zf_fwd), lb_f)
    o_bwd = flip(hgrn2_direction(flip(q), flip(v), flip(heads(zf_bwd)), lb_b))
    return (o_fwd + o_bwd).reshape(B, S, D_HGRN)


def token_mixer(h, w_in, lb_fwd, lb_bwd, fourier_g, hgrn_g, w_out):
    proj = h @ w_in
    cuts = [D_FOURIER + j * D_HGRN for j in range(5)]
    u_f, q_raw, v_raw, zf_fwd, zf_bwd, z_gate = jnp.split(proj, cuts, axis=-1)
    y_f = rms_norm_heads(fourier_mix(u_f), fourier_g, FOURIER_GROUPS)
    o_h = bidirectional_hgrn2(q_raw, v_raw, zf_fwd, zf_bwd, lb_fwd, lb_bwd)
    y_h = rms_norm_heads(o_h, hgrn_g, HG_HEADS) * jax.nn.silu(z_gate.astype(jnp.float32))
    y = jnp.concatenate([y_f, y_h], axis=-1).astype(h.dtype)
    return y @ w_out


def setup_inputs(seed: int = 0) -> dict:
    key = jax.random.key(seed)
    ks = jax.random.split(key, 24)

    def nrm(k, shape, scale):
        return jax.random.normal(k, shape, jnp.float32) * scale

    return {
        "x": nrm(ks[0], (BATCH, SEQ, D_MODEL), 1.0),
        "c": nrm(ks[1], (BATCH, D_MODEL), 1.0),
        "w_ada": nrm(ks[2], (DEPTH, D_MODEL, N_MOD * D_MODEL), ADA_INIT * D_MODEL ** -0.5),
        "b_ada": nrm(ks[3], (DEPTH, N_MOD * D_MODEL), 0.01),
        "w_ffn1_in": nrm(ks[4], (DEPTH, D_MODEL, 2 * D_FF), D_MODEL ** -0.5),
        "w_ffn1_out": nrm(ks[5], (DEPTH, D_FF, D_MODEL), BETA * D_FF ** -0.5),
        "ln1_g": 1.0 + nrm(ks[6], (DEPTH, D_MODEL), 0.02),
        "ln1_b": nrm(ks[7], (DEPTH, D_MODEL), 0.02),
        "w_in": nrm(ks[8], (DEPTH, D_MODEL, D_IN), D_MODEL ** -0.5),
        "lower_bounds": nrm(ks[9], (DEPTH, 2, D_HGRN), 0.1),
        "fourier_g": 1.0 + nrm(ks[10], (DEPTH, D_FOURIER), 0.02),
        "hgrn_g": 1.0 + nrm(ks[11], (DEPTH, D_HGRN), 0.02),
        "w_out": nrm(ks[12], (DEPTH, D_MIX, D_MODEL), BETA * D_MIX ** -0.5),
        "ln2_g": 1.0 + nrm(ks[13], (DEPTH, D_MODEL), 0.02),
        "ln2_b": nrm(ks[14], (DEPTH, D_MODEL), 0.02),
        "w_ffn2_in": nrm(ks[15], (DEPTH, D_MODEL, 2 * D_FF), D_MODEL ** -0.5),
        "w_ffn2_out": nrm(ks[16], (DEPTH, D_FF, D_MODEL), BETA * D_FF ** -0.5),
        "ln3_g": 1.0 + nrm(ks[17], (DEPTH, D_MODEL), 0.02),
        "ln3_b": nrm(ks[18], (DEPTH, D_MODEL), 0.02),
    }


def reference(x, c, w_ada, b_ada, w_ffn1_in, w_ffn1_out, ln1_g, ln1_b, w_in, lower_bounds,
              fourier_g, hgrn_g, w_out, ln2_g, ln2_b, w_ffn2_in, w_ffn2_out, ln3_g, ln3_b):
    lb_soft = jax.nn.softmax(lower_bounds.astype(jnp.float32), axis=0)
    lbs = jnp.cumsum(lb_soft, axis=0) - lb_soft[0:1]
    c_act = jax.nn.silu(c)
    for l in range(DEPTH):
        ada = (c_act @ w_ada[l] + b_ada[l]).astype(x.dtype)
        sh1, sc1, g1, sh2, sc2, g2, sh3, sc3, g3 = jnp.split(ada, N_MOD, axis=-1)
        y = swiglu(modulate(x, sh1, sc1), w_ffn1_in[l], w_ffn1_out[l])
        x = layer_norm(ALPHA * x + 0.5 * (1.0 + g1[:, None, :]) * y, ln1_g[l], ln1_b[l])
        y = token_mixer(modulate(x, sh2, sc2), w_in[l], lbs[l, 0], lbs[l, 1],
                        fourier_g[l], hgrn_g[l], w_out[l])
        x = layer_norm(ALPHA * x + (1.0 + g2[:, None, :]) * y, ln2_g[l], ln2_b[l])
        y = swiglu(modulate(x, sh3, sc3), w_ffn2_in[l], w_ffn2_out[l])
        x = layer_norm(ALPHA * x + 0.5 * (1.0 + g3[:, None, :]) * y, ln3_g[l], ln3_b[l])
    return x
```

```python
import functools

import jax
import jax.numpy as jnp
from jax import lax
from jax.experimental import pallas as pl
from jax.experimental.pallas import tpu as pltpu

F32 = jnp.float32
BF16 = jnp.bfloat16

LN_EPS = 1e-5
RMS_EPS = 1e-6
CHUNK = 64
LANES = 128
N_MOD = 9
FOURIER_GROUPS = 4
FF_ALIGN = 512
VMEM_LIMIT = 56 * 1024 * 1024


def _params(*sem):
    return pltpu.CompilerParams(dimension_semantics=sem, vmem_limit_bytes=VMEM_LIMIT)


def _sigmoid(z):
    return 1.0 / (1.0 + jnp.exp(-z))


def _silu(z):
    return z * _sigmoid(z)


def _layer_norm(xf, g, b):
    mu = jnp.mean(xf, axis=-1, keepdims=True)
    xc = xf - mu
    var = jnp.mean(xc * xc, axis=-1, keepdims=True)
    return xc * lax.rsqrt(var + LN_EPS) * g + b


def _dot(a, b):
    return jnp.dot(a, b, preferred_element_type=F32)


def _dot_nt(a, b):
    return lax.dot_general(a, b, (((1,), (1,)), ((), ())), preferred_element_type=F32)


def _ada_kernel(c_ref, w_ref, b_ref, o_ref):
    c = c_ref[...]
    o_ref[0] = _dot(_silu(c).astype(BF16), w_ref[0].astype(BF16)) + b_ref[0]


def _ada_proj(c, w_ada, b_ada):
    depth, d, n = w_ada.shape
    bsz = c.shape[0]
    tn = min(d, 1024)
    assert n % tn == 0
    return pl.pallas_call(
        _ada_kernel,
        grid=(depth, n // tn),
        in_specs=[
            pl.BlockSpec((bsz, d), lambda l, j: (0, 0)),
            pl.BlockSpec((1, d, tn), lambda l, j: (l, 0, j)),
            pl.BlockSpec((1, 1, tn), lambda l, j: (l, 0, j)),
        ],
        out_specs=pl.BlockSpec((1, bsz, tn), lambda l, j: (l, 0, j)),
        out_shape=jax.ShapeDtypeStruct((depth, bsz, n), F32),
        compiler_params=_params("parallel", "parallel"),
        name="ada_proj",
    )(c, w_ada, b_ada.reshape(depth, 1, n))


def _ffn_kernel(x_ref, ada_ref, wa_ref, wb_ref, wo_ref, g_ref, b_ref, o_ref, h_ref, acc_ref,
                *, mod0, alpha):
    j = pl.program_id(2)

    @pl.when(j == 0)
    def _():
        shift = ada_ref[mod0:mod0 + 1, :]
        scale = ada_ref[mod0 + 1:mod0 + 2, :]
        h_ref[...] = (x_ref[...] * (1.0 + scale) + shift).astype(BF16)
        acc_ref[...] = jnp.zeros_like(acc_ref)

    h = h_ref[...]
    a = _dot(h, wa_ref[...])
    b = _dot(h, wb_ref[...])
    acc_ref[...] += _dot((_silu(a) * b).astype(BF16), wo_ref[...])

    @pl.when(j == pl.num_programs(2) - 1)
    def _():
        gate = ada_ref[mod0 + 2:mod0 + 3, :]
        y = alpha * x_ref[...] + (0.5 * (1.0 + gate)) * acc_ref[...]
        o_ref[...] = _layer_norm(y, g_ref[...], b_ref[...])


def _ffn(x, ada_l, w_in, w_out, ln_g, ln_b, *, mod0, alpha, tm, tf):
    bsz, s, d = x.shape
    fp = w_out.shape[0]
    nj = fp // tf
    kern = functools.partial(_ffn_kernel, mod0=mod0, alpha=alpha)
    return pl.pallas_call(
        kern,
        grid=(bsz, s // tm, nj),
        in_specs=[
            pl.BlockSpec((None, tm, d), lambda b, i, j: (b, i, 0)),
            pl.BlockSpec((None, N_MOD, d), lambda b, i, j: (b, 0, 0)),
            pl.BlockSpec((d, tf), lambda b, i, j: (0, j)),
            pl.BlockSpec((d, tf), lambda b, i, j: (0, j + nj)),
            pl.BlockSpec((tf, d), lambda b, i, j: (j, 0)),
            pl.BlockSpec((1, d), lambda b, i, j: (0, 0)),
            pl.BlockSpec((1, d), lambda b, i, j: (0, 0)),
        ],
        out_specs=pl.BlockSpec((None, tm, d), lambda b, i, j: (b, i, 0)),
        out_shape=jax.ShapeDtypeStruct((bsz, s, d), F32),
        scratch_shapes=[pltpu.VMEM((tm, d), BF16), pltpu.VMEM((tm, d), F32)],
        compiler_params=_params("parallel", "parallel", "arbitrary"),
        name="ffn",
    )(x, ada_l, w_in, w_in, w_out, ln_g, ln_b)


def _in_proj_kernel(x_ref, ada_ref, w_ref, o_ref, h_ref, *, mod0):
    @pl.when(pl.program_id(2) == 0)
    def _():
        shift = ada_ref[mod0:mod0 + 1, :]
        scale = ada_ref[mod0 + 1:mod0 + 2, :]
        h_ref[...] = (x_ref[...] * (1.0 + scale) + shift).astype(BF16)

    res = _dot(h_ref[...], w_ref[...])
    for c in range(o_ref.shape[0]):
        o_ref[c] = res[:, c * LANES:(c + 1) * LANES].astype(BF16)


def _in_proj(x, ada_l, w, *, mod0, tm, tn):
    bsz, s, d = x.shape
    n = w.shape[1]
    kern = functools.partial(_in_proj_kernel, mod0=mod0)
    return pl.pallas_call(
        kern,
        grid=(bsz, s // tm, n // tn),
        in_specs=[
            pl.BlockSpec((None, tm, d), lambda b, i, j: (b, i, 0)),
            pl.BlockSpec((None, N_MOD, d), lambda b, i, j: (b, 0, 0)),
            pl.BlockSpec((d, tn), lambda b, i, j: (0, j)),
        ],
        out_specs=pl.BlockSpec((None, tn // LANES, tm, LANES), lambda b, i, j: (b, j, i, 0)),
        out_shape=jax.ShapeDtypeStruct((bsz, n // LANES, s, LANES), BF16),
        scratch_shapes=[pltpu.VMEM((tm, d), BF16)],
        compiler_params=_params("parallel", "parallel", "arbitrary"),
        name="in_proj",
    )(x, ada_l, w)


def _chan_dft_kernel(u_ref, w_ref, o_ref):
    res = _dot(u_ref[...], w_ref[...])
    o_ref[0] = res[:, :LANES].astype(BF16)
    o_ref[1] = res[:, LANES:].astype(BF16)


def _chan_dft(proj, w_chan, *, tm):
    bsz, _, s, _ = proj.shape
    return pl.pallas_call(
        _chan_dft_kernel,
        grid=(bsz, FOURIER_GROUPS, s // tm),
        in_specs=[
            pl.BlockSpec((None, None, tm, LANES), lambda b, g, i: (b, g, i, 0)),
            pl.BlockSpec((LANES, 2 * LANES), lambda b, g, i: (0, 0)),
        ],
        out_specs=pl.BlockSpec((2, tm, LANES), lambda b, g, i: (0, i, b * FOURIER_GROUPS + g)),
        out_shape=jax.ShapeDtypeStruct((2, s, bsz * FOURIER_GROUPS * LANES), BF16),
        compiler_params=_params("parallel", "parallel", "parallel"),
        name="chan_dft",
    )(proj, w_chan)


def _seq_dft_kernel(m_ref, ab_ref, g_ref, o_ref, acc_ref, *, scale, nb):
    k = pl.program_id(2)

    @pl.when(k == 0)
    def _():
        acc_ref[...] = jnp.zeros_like(acc_ref)

    acc_ref[...] += _dot(m_ref[...], ab_ref[...])

    @pl.when(k == pl.num_programs(2) - 1)
    def _():
        for bb in range(nb):
            for g in range(FOURIER_GROUPS):
                c0 = (bb * FOURIER_GROUPS + g) * LANES
                y = acc_ref[:, c0:c0 + LANES] * scale
                y = y * lax.rsqrt(jnp.mean(y * y, axis=-1, keepdims=True) + RMS_EPS)
                o_ref[bb, g] = (y * g_ref[:, g * LANES:(g + 1) * LANES]).astype(BF16)


def _seq_dft(dft_mat, ab, gain, *, bsz, scale, tm, nb, tk):
    s = dft_mat.shape[0]
    k_total = dft_mat.shape[1]
    tn = nb * FOURIER_GROUPS * LANES
    kern = functools.partial(_seq_dft_kernel, scale=scale, nb=nb)
    return pl.pallas_call(
        kern,
        grid=(bsz // nb, s // tm, k_total // tk),
        in_specs=[
            pl.BlockSpec((tm, tk), lambda n, i, k: (i, k)),
            pl.BlockSpec((tk, tn), lambda n, i, k: (k, n)),
            pl.BlockSpec((1, FOURIER_GROUPS * LANES), lambda n, i, k: (0, 0)),
        ],
        out_specs=pl.BlockSpec((nb, FOURIER_GROUPS, tm, LANES), lambda n, i, k: (n, 0, i, 0)),
        out_shape=jax.ShapeDtypeStruct((bsz, FOURIER_GROUPS, s, LANES), BF16),
        scratch_shapes=[pltpu.VMEM((tm, tn), F32)],
        compiler_params=_params("parallel", "parallel", "arbitrary"),
        name="seq_dft",
    )(dft_mat, ab, gain)


def _split_dot(tri, x):
    hi = x.astype(BF16)
    lo = (x - hi.astype(F32)).astype(BF16)
    return _dot(tri, hi) + _dot(tri, lo)


def _gla_kernel(q_ref, v_ref, zf_ref, zb_ref, gt_ref, lb_ref, gain_ref, o_ref,
                pb_ref, qsb_ref, sb_all_ref, st_ref, *, layer, group):
    s = q_ref.shape[0]
    n_chunks = s // CHUNK
    n_groups = n_chunks // group

    lb_raw = lb_ref[...]
    lb_exp = jnp.exp(lb_raw - jnp.max(lb_raw, axis=0, keepdims=True))
    lb_soft = lb_exp / jnp.sum(lb_exp, axis=0, keepdims=True)
    lbs = jnp.sum(lb_soft[:layer + 1], axis=0) - lb_soft[0]
    lb_f = lbs[0:1, :]
    lb_b = lbs[1:2, :]

    row = lax.broadcasted_iota(jnp.int32, (CHUNK, CHUNK), 0)
    col = lax.broadcasted_iota(jnp.int32, (CHUNK, CHUNK), 1)
    lower = row >= col
    upper = row <= col
    tri_lo = lower.astype(BF16)
    tri_up = upper.astype(BF16)

    def gate_terms(z_ref, r, lb):
        z = z_ref[pl.ds(r, CHUNK), :].astype(F32)
        f = lb + (1.0 - lb) * _sigmoid(z)
        return 1.0 - f, jnp.log(f)

    def scaled(q, k, b, i_ref, i_last):
        b_ref = b[i_ref:i_ref + 1, :]
        b_last = b[i_last:i_last + 1, :]
        q_in = q * jnp.exp(b - b_ref)
        k_in = k * jnp.exp(b_ref - b)
        q_st = q_in * jnp.exp(b_ref)
        k_st = k_in * jnp.exp(b_last - b_ref)
        return q_in, k_in, q_st, k_st, jnp.exp(b_last)

    st_ref[...] = jnp.zeros_like(st_ref)

    def bwd_group(gi, carry):
        for u in range(group):
            c = n_chunks - 1 - (gi * group + u)
            r = pl.multiple_of(c * CHUNK, CHUNK)
            q = _silu(q_ref[pl.ds(r, CHUNK), :].astype(F32))
            v = v_ref[pl.ds(r, CHUNK), :].astype(F32)
            k, logf = gate_terms(zb_ref, r, lb_b)
            b = _split_dot(tri_up, logf)
            q_in, k_in, q_st, k_st, decay = scaled(q, k, b, CHUNK - 1 - CHUNK // 2, 0)
            scores = _dot_nt(q_in.astype(BF16), k_in.astype(BF16))
            pb_ref[c] = jnp.where(upper, scores, 0.0).astype(BF16)
            qsb_ref[pl.ds(r, CHUNK), :] = q_st.astype(BF16)
            state = st_ref[...]
            sb_all_ref[c] = state.astype(BF16)
            st_ref[...] = decay * state + _dot(v.T.astype(BF16), k_st.astype(BF16))
        return carry

    lax.fori_loop(0, n_groups, bwd_group, 0)

    st_ref[...] = jnp.zeros_like(st_ref)
    gain = gain_ref[...]

    def fwd_group(gi, carry):
        for u in range(group):
            c = gi * group + u
            r = pl.multiple_of(c * CHUNK, CHUNK)
            q = _silu(q_ref[pl.ds(r, CHUNK), :].astype(F32))
            v = v_ref[pl.ds(r, CHUNK), :].astype(F32)
            k, logf = gate_terms(zf_ref, r, lb_f)
            b = _split_dot(tri_lo, logf)
            q_in, k_in, q_st, k_st, decay = scaled(q, k, b, CHUNK // 2, CHUNK - 1)
            scores = _dot_nt(q_in.astype(BF16), k_in.astype(BF16))
            p = jnp.where(lower, scores, 0.0) + pb_ref[c].astype(F32)
            state = st_ref[...]
            v16 = v.astype(BF16)
            o = _dot(p.astype(BF16), v16)
            o = o + _dot_nt(q_st.astype(BF16), state.astype(BF16))
            o = o + _dot_nt(qsb_ref[pl.ds(r, CHUNK), :], sb_all_ref[c])
            st_ref[...] = decay * state + _dot(v.T.astype(BF16), k_st.astype(BF16))
            y = o * lax.rsqrt(jnp.mean(o * o, axis=-1, keepdims=True) + RMS_EPS) * gain
            y = y * _silu(gt_ref[pl.ds(r, CHUNK), :].astype(F32))
            o_ref[pl.ds(r, CHUNK), :] = y.astype(BF16)
        return carry

    lax.fori_loop(0, n_groups, fwd_group, 0)


def _gla(proj, lower_bounds, hgrn_g_l, *, layer, n_heads, group):
    bsz, _, s, _ = proj.shape
    depth = lower_bounds.shape[0]
    n_chunks = s // CHUNK
    base = FOURIER_GROUPS

    def slab(j):
        return pl.BlockSpec((None, None, s, LANES), lambda b, h: (b, base + j * n_heads + h, 0, 0))

    kern = functools.partial(_gla_kernel, layer=layer, group=group)
    return pl.pallas_call(
        kern,
        grid=(bsz, n_heads),
        in_specs=[
            slab(0), slab(1), slab(2), slab(3), slab(4),
            pl.BlockSpec((depth, 2, LANES), lambda b, h: (0, 0, h)),
            pl.BlockSpec((1, LANES), lambda b, h: (0, h)),
        ],
        out_specs=pl.BlockSpec((None, None, s, LANES), lambda b, h: (b, h, 0, 0)),
        out_shape=jax.ShapeDtypeStruct((bsz, n_heads, s, LANES), BF16),
        scratch_shapes=[
            pltpu.VMEM((n_chunks, CHUNK, CHUNK), BF16),
            pltpu.VMEM((s, LANES), BF16),
            pltpu.VMEM((n_chunks, LANES, LANES), BF16),
            pltpu.VMEM((LANES, LANES), F32),
        ],
        compiler_params=_params("parallel", "parallel"),
        name="gla",
    )(proj, proj, proj, proj, proj, lower_bounds, hgrn_g_l)


def _out_proj_kernel(yf_ref, yh_ref, x_ref, ada_ref, w_ref, g_ref, b_ref, o_ref, *, mod0, alpha):
    parts = [yf_ref[g] for g in range(yf_ref.shape[0])] + [yh_ref[h] for h in range(yh_ref.shape[0])]
    y = _dot(jnp.concatenate(parts, axis=-1), w_ref[...])
    gate = ada_ref[mod0 + 2:mod0 + 3, :]
    o_ref[...] = _layer_norm(alpha * x_ref[...] + (1.0 + gate) * y, g_ref[...], b_ref[...])


def _out_proj(yf, yh, x, ada_l, w, ln_g, ln_b, *, mod0, alpha, tm):
    bsz, s, d = x.shape
    n_heads = yh.shape[1]
    kern = functools.partial(_out_proj_kernel, mod0=mod0, alpha=alpha)
    return pl.pallas_call(
        kern,
        grid=(bsz, s // tm),
        in_specs=[
            pl.BlockSpec((None, FOURIER_GROUPS, tm, LANES), lambda b, i: (b, 0, i, 0)),
            pl.BlockSpec((None, n_heads, tm, LANES), lambda b, i: (b, 0, i, 0)),
            pl.BlockSpec((None, tm, d), lambda b, i: (b, i, 0)),
            pl.BlockSpec((None, N_MOD, d), lambda b, i: (b, 0, 0)),
            pl.BlockSpec(w.shape, lambda b, i: (0, 0)),
            pl.BlockSpec((1, d), lambda b, i: (0, 0)),
            pl.BlockSpec((1, d), lambda b, i: (0, 0)),
        ],
        out_specs=pl.BlockSpec((None, tm, d), lambda b, i: (b, i, 0)),
        out_shape=jax.ShapeDtypeStruct((bsz, s, d), F32),
        compiler_params=_params("parallel", "parallel"),
        name="out_proj",
    )(yf, yh, x, ada_l, w, ln_g, ln_b)


def _dft_tables(s):
    k = lax.broadcasted_iota(jnp.int32, (s, s), 0)
    n = lax.broadcasted_iota(jnp.int32, (s, s), 1)
    ang = ((k * n) % s).astype(F32) * (2.0 * jnp.pi / s)
    return jnp.cos(ang), jnp.sin(ang)


def _pad_ffn_weights(w_in, w_out):
    d_ff = w_out.shape[0]
    fp = -(-d_ff // FF_ALIGN) * FF_ALIGN
    pad = fp - d_ff
    wa = jnp.pad(w_in[:, :d_ff], ((0, 0), (0, pad)))
    wb = jnp.pad(w_in[:, d_ff:], ((0, 0), (0, pad)))
    return (jnp.concatenate([wa, wb], axis=1).astype(BF16),
            jnp.pad(w_out, ((0, pad), (0, 0))).astype(BF16))


def kernel(x, c, w_ada, b_ada, w_ffn1_in, w_ffn1_out, ln1_g, ln1_b, w_in, lower_bounds, fourier_g, hgrn_g,
           w_out, ln2_g, ln2_b, w_ffn2_in, w_ffn2_out, ln3_g, ln3_b):
    bsz, s, d = x.shape
    depth = w_ada.shape[0]
    d_fourier = fourier_g.shape[1]
    d_hgrn = hgrn_g.shape[1]
    n_heads = d_hgrn // LANES
    assert d_fourier == FOURIER_GROUPS * LANES and w_in.shape[2] == d_fourier + 5 * d_hgrn
    assert s % CHUNK == 0
    alpha = float((2 * depth) ** 0.25)

    tm = min(s, 512)
    tn_proj = min(w_in.shape[2], 1024)
    gla_group = 4 if (s // CHUNK) % 4 == 0 else 1
    nb = 2 if bsz % 2 == 0 else 1
    tm_dft = min(s, 1024)
    tk_dft = min(2 * s, 2048)

    ada = _ada_proj(c, w_ada, b_ada).reshape(depth, bsz, N_MOD, d)

    cos_s, sin_s = _dft_tables(s)
    dft_mat = jnp.concatenate([cos_s, sin_s], axis=1).astype(BF16)
    cos_c, sin_c = _dft_tables(LANES)
    w_chan = jnp.concatenate([cos_c, -sin_c], axis=1).astype(BF16)
    dft_scale = float((s * LANES) ** -0.5)

    for l in range(depth):
        ada_l = ada[l]
        wi1, wo1 = _pad_ffn_weights(w_ffn1_in[l], w_ffn1_out[l])
        wi2, wo2 = _pad_ffn_weights(w_ffn2_in[l], w_ffn2_out[l])

        x = _ffn(x, ada_l, wi1, wo1, ln1_g[l][None], ln1_b[l][None], mod0=0, alpha=alpha, tm=tm, tf=FF_ALIGN)

        proj = _in_proj(x, ada_l, w_in[l].astype(BF16), mod0=3, tm=tm, tn=tn_proj)
        ab = _chan_dft(proj, w_chan, tm=tm).reshape(2 * s, bsz * d_fourier)
        yf = _seq_dft(dft_mat, ab, fourier_g[l][None], bsz=bsz, scale=dft_scale, tm=tm_dft, nb=nb, tk=tk_dft)
        yh = _gla(proj, lower_bounds, hgrn_g[l][None], layer=l, n_heads=n_heads, group=gla_group)
        x = _out_proj(yf, yh, x, ada_l, w_out[l].astype(BF16), ln2_g[l][None], ln2_b[l][None],
                      mod0=3, alpha=alpha, tm=tm)

        x = _ffn(x, ada_l, wi2, wo2, ln3_g[l][None], ln3_b[l][None], mod0=6, alpha=alpha, tm=tm, tf=FF_ALIGN)
    return x
```

```python
import functools

import jax
import jax.numpy as jnp
from jax import lax
from jax.experimental import pallas as pl
from jax.experimental.pallas import tpu as pltpu

F32 = jnp.float32
BF16 = jnp.bfloat16

LN_EPS = 1e-5
RMS_EPS = 1e-6
CHUNK = 64
LANES = 128
N_MOD = 9
FOURIER_GROUPS = 4
D_FOURIER = FOURIER_GROUPS * LANES
FF_ALIGN = 512
GLA_GROUP = 4
GLA_OUT_GROUP = 8
PROJ_SUB_SLABS = 4
VMEM_LIMIT = 60 * 1024 * 1024


def _params(*sem):
    return pltpu.CompilerParams(dimension_semantics=sem, vmem_limit_bytes=VMEM_LIMIT)


def _sigmoid(z):
    return 1.0 / (1.0 + jnp.exp(-z))


def _silu(z):
    return z * _sigmoid(z)


def _layer_norm(xf, g, b):
    mu = jnp.mean(xf, axis=-1, keepdims=True)
    xc = xf - mu
    var = jnp.mean(xc * xc, axis=-1, keepdims=True)
    return xc * lax.rsqrt(var + LN_EPS) * g + b


def _dot(a, b):
    return jnp.dot(a, b, preferred_element_type=F32)


def _dot_nt(a, b):
    return lax.dot_general(a, b, (((1,), (1,)), ((), ())), preferred_element_type=F32)


def _ada_kernel(c_ref, w_ref, b_ref, o_ref):
    c = c_ref[...]
    o_ref[0] = _dot(_silu(c).astype(BF16), w_ref[0].astype(BF16)) + b_ref[0]


def _ada_proj(c, w_ada, b_ada):
    depth, d, n = w_ada.shape
    bsz = c.shape[0]
    tn = min(d, 1024)
    assert n % tn == 0
    return pl.pallas_call(
        _ada_kernel,
        grid=(depth, n // tn),
        in_specs=[
            pl.BlockSpec((bsz, d), lambda l, j: (0, 0)),
            pl.BlockSpec((1, d, tn), lambda l, j: (l, 0, j)),
            pl.BlockSpec((1, 1, tn), lambda l, j: (l, 0, j)),
        ],
        out_specs=pl.BlockSpec((1, bsz, tn), lambda l, j: (l, 0, j)),
        out_shape=jax.ShapeDtypeStruct((depth, bsz, n), F32),
        compiler_params=_params("parallel", "parallel"),
        name="ada_proj",
    )(c, w_ada, b_ada.reshape(depth, 1, n))


def _ffn_kernel(x_ref, ada_ref, wa_ref, wb_ref, wo_ref, g_ref, b_ref, o_ref, h_ref, *, mod0, alpha):
    j = pl.program_id(2)

    @pl.when(j == 0)
    def _():
        shift = ada_ref[mod0:mod0 + 1, :]
        scale = ada_ref[mod0 + 1:mod0 + 2, :]
        h_ref[...] = (x_ref[...] * (1.0 + scale) + shift).astype(BF16)
        o_ref[...] = jnp.zeros_like(o_ref)

    h = h_ref[...]
    a = _dot(h, wa_ref[...])
    b = _dot(h, wb_ref[...])
    o_ref[...] += _dot((_silu(a) * b).astype(BF16), wo_ref[...])

    @pl.when(j == pl.num_programs(2) - 1)
    def _():
        gate = ada_ref[mod0 + 2:mod0 + 3, :]
        y_res = alpha * x_ref[...] + (0.5 * (1.0 + gate)) * o_ref[...]
        o_ref[...] = _layer_norm(y_res, g_ref[...], b_ref[...])


def _ffn(x, ada_l, w_in, w_out, ln_g, ln_b, *, mod0, alpha, tm, tf):
    bsz, s, d = x.shape
    fp = w_out.shape[0]
    nj = fp // tf
    kern = functools.partial(_ffn_kernel, mod0=mod0, alpha=alpha)
    return pl.pallas_call(
        kern,
        grid=(bsz, s // tm, nj),
        in_specs=[
            pl.BlockSpec((None, tm, d), lambda b, i, j: (b, i, 0), pipeline_mode=pl.Buffered(1)),
            pl.BlockSpec((None, N_MOD, d), lambda b, i, j: (b, 0, 0)),
            pl.BlockSpec((d, tf), lambda b, i, j: (0, j)),
            pl.BlockSpec((d, tf), lambda b, i, j: (0, j + nj)),
            pl.BlockSpec((tf, d), lambda b, i, j: (j, 0)),
            pl.BlockSpec((1, d), lambda b, i, j: (0, 0)),
            pl.BlockSpec((1, d), lambda b, i, j: (0, 0)),
        ],
        out_specs=pl.BlockSpec((None, tm, d), lambda b, i, j: (b, i, 0)),
        out_shape=jax.ShapeDtypeStruct((bsz, s, d), F32),
        scratch_shapes=[pltpu.VMEM((tm, d), BF16)],
        compiler_params=_params("parallel", "parallel", "arbitrary"),
        name="ffn",
    )(x, ada_l, w_in, w_in, w_out, ln_g, ln_b)


def _in_proj_kernel(x_ref, ada_ref, w_ref, wc_ref, o_ref, ab_ref, h_ref, *, mod0, n_silu):
    j = pl.program_id(2)

    @pl.when(j == 0)
    def _():
        shift = ada_ref[mod0:mod0 + 1, :]
        scale = ada_ref[mod0 + 1:mod0 + 2, :]
        h_ref[...] = (x_ref[...] * (1.0 + scale) + shift).astype(BF16)

    is_silu = j < n_silu
    n_slabs = o_ref.shape[0]
    sub = min(n_slabs, PROJ_SUB_SLABS)
    for c0 in range(0, n_slabs, sub):
        res = _dot(h_ref[...], w_ref[:, c0 * LANES:(c0 + sub) * LANES])
        res = jnp.where(is_silu, _silu(res), res)
        for c in range(sub):
            o_ref[c0 + c] = res[:, c * LANES:(c + 1) * LANES].astype(BF16)

    @pl.when(j == pl.num_programs(2) - 1)
    def _():
        for g in range(FOURIER_GROUPS):
            u = o_ref[n_slabs - FOURIER_GROUPS + g]
            cs = _dot(u, wc_ref[...])
            ab_ref[0, :, g * LANES:(g + 1) * LANES] = cs[:, :LANES].astype(BF16)
            ab_ref[1, :, g * LANES:(g + 1) * LANES] = cs[:, LANES:].astype(BF16)


def _in_proj(x, ada_l, w, w_chan, *, mod0, tm, tn, n_silu):
    bsz, s, d = x.shape
    n = w.shape[1]
    kern = functools.partial(_in_proj_kernel, mod0=mod0, n_silu=n_silu)
    return pl.pallas_call(
        kern,
        grid=(bsz, s // tm, n // tn),
        in_specs=[
            pl.BlockSpec((None, tm, d), lambda b, i, j: (b, i, 0)),
            pl.BlockSpec((None, N_MOD, d), lambda b, i, j: (b, 0, 0)),
            pl.BlockSpec((d, tn), lambda b, i, j: (0, j)),
            pl.BlockSpec((LANES, 2 * LANES), lambda b, i, j: (0, 0)),
        ],
        out_specs=[
            pl.BlockSpec((None, tn // LANES, tm, LANES), lambda b, i, j: (b, j, i, 0)),
            pl.BlockSpec((2, tm, D_FOURIER), lambda b, i, j: (0, i, b)),
        ],
        out_shape=[
            jax.ShapeDtypeStruct((bsz, n // LANES, s, LANES), BF16),
            jax.ShapeDtypeStruct((2, s, bsz * D_FOURIER), BF16),
        ],
        scratch_shapes=[pltpu.VMEM((tm, d), BF16)],
        compiler_params=_params("parallel", "parallel", "arbitrary"),
        name="in_proj",
    )(x, ada_l, w, w_chan)


def _seq_dft_kernel(m_ref, ab_ref, g_ref, o_ref, acc_ref, *, scale, nb):
    k = pl.program_id(2)

    @pl.when(k == 0)
    def _():
        acc_ref[...] = jnp.zeros_like(acc_ref)

    acc_ref[...] += _dot(m_ref[...], ab_ref[...])

    @pl.when(k == pl.num_programs(2) - 1)
    def _():
        for bb in range(nb):
            for g in range(FOURIER_GROUPS):
                c0 = (bb * FOURIER_GROUPS + g) * LANES
                y = acc_ref[:, c0:c0 + LANES] * scale
                y = y * lax.rsqrt(jnp.mean(y * y, axis=-1, keepdims=True) + RMS_EPS)
                o_ref[bb, g] = (y * g_ref[:, g * LANES:(g + 1) * LANES]).astype(BF16)


def _seq_dft(dft_mat, ab, gain, *, bsz, scale, tm, nb, tk):
    s = dft_mat.shape[0]
    k_total = dft_mat.shape[1]
    tn = nb * D_FOURIER
    kern = functools.partial(_seq_dft_kernel, scale=scale, nb=nb)
    return pl.pallas_call(
        kern,
        grid=(bsz // nb, s // tm, k_total // tk),
        in_specs=[
            pl.BlockSpec((tm, tk), lambda n, i, k: (i, k)),
            pl.BlockSpec((tk, tn), lambda n, i, k: (k, n)),
            pl.BlockSpec((1, D_FOURIER), lambda n, i, k: (0, 0)),
        ],
        out_specs=pl.BlockSpec((nb, FOURIER_GROUPS, tm, LANES), lambda n, i, k: (n, 0, i, 0)),
        out_shape=jax.ShapeDtypeStruct((bsz, FOURIER_GROUPS, s, LANES), BF16),
        scratch_shapes=[pltpu.VMEM((tm, tn), F32)],
        compiler_params=_params("parallel", "parallel", "arbitrary"),
        name="seq_dft",
    )(dft_mat, ab, gain)


def _split_cumsum(tri, x):
    hi = x.astype(BF16)
    lo = (x - hi.astype(F32)).astype(BF16)
    both = _dot(tri, jnp.concatenate([hi, lo], axis=1))
    return both[:, :LANES] + both[:, LANES:]


def _gla_kernel(q_ref, gt_ref, v_ref, zf_ref, zb_ref, lb_ref, gain_ref, tl_ref, tu_ref, o_ref,
                p_ref, kv_ref, qst_ref, sall_ref, d_ref, *, layer, group, out_group):
    s = q_ref.shape[0]
    n_chunks = s // CHUNK
    rows_g = group * CHUNK
    n_groups = n_chunks // group

    lb_raw = lb_ref[...]
    lb_exp = jnp.exp(lb_raw - jnp.max(lb_raw, axis=0, keepdims=True))
    lb_soft = lb_exp / jnp.sum(lb_exp, axis=0, keepdims=True)
    lbs = jnp.sum(lb_soft[:layer + 1], axis=0) - lb_soft[0]
    lb_f = lbs[0:1, :]
    lb_b = lbs[1:2, :]

    row = lax.broadcasted_iota(jnp.int32, (CHUNK, CHUNK), 0)
    col = lax.broadcasted_iota(jnp.int32, (CHUNK, CHUNK), 1)
    lower = row >= col
    upper = row <= col

    def gate_terms(z_ref, rows, lb):
        z = z_ref[rows, :].astype(F32)
        f = lb + (1.0 - lb) * _sigmoid(z)
        return 1.0 - f, jnp.log(f)

    def scaled(q3, k3, b3, i_ref, i_last):
        b_ref = b3[:, i_ref:i_ref + 1, :]
        b_last = b3[:, i_last:i_last + 1, :]
        q_in = q3 * jnp.exp(b3 - b_ref)
        k_in = k3 * jnp.exp(b_ref - b3)
        q_st = q_in * jnp.exp(b_ref)
        k_st = k_in * jnp.exp(b_last - b_ref)
        return q_in, k_in, q_st, k_st, jnp.exp(b_last)

    def pass0(gi, carry):
        r0 = pl.multiple_of(gi * rows_g, rows_g)
        rows = pl.ds(r0, rows_g)
        q3 = q_ref[rows, :].astype(F32).reshape(group, CHUNK, LANES)
        v = v_ref[rows, :].astype(F32)
        k_f, logf_f = gate_terms(zf_ref, rows, lb_f)
        k_b, logf_b = gate_terms(zb_ref, rows, lb_b)
        b_f = _split_cumsum(tl_ref[...], logf_f)
        b_b = _split_cumsum(tu_ref[...], logf_b)
        shape3 = (group, CHUNK, LANES)
        qf_in, kf_in, qf_st, kf_st, d_f = scaled(q3, k_f.reshape(shape3), b_f.reshape(shape3),
                                                 CHUNK // 2, CHUNK - 1)
        qb_in, kb_in, qb_st, kb_st, d_b = scaled(q3, k_b.reshape(shape3), b_b.reshape(shape3),
                                                 CHUNK - 1 - CHUNK // 2, 0)
        qst = jnp.concatenate([qf_st.reshape(rows_g, LANES), qb_st.reshape(rows_g, LANES)], axis=1)
        qst_ref[rows, :] = qst.astype(BF16)
        for g in range(group):
            c = gi * group + g
            s_f = _dot_nt(qf_in[g].astype(BF16), kf_in[g].astype(BF16))
            s_b = _dot_nt(qb_in[g].astype(BF16), kb_in[g].astype(BF16))
            p_ref[c] = (jnp.where(lower, s_f, 0.0) + jnp.where(upper, s_b, 0.0)).astype(BF16)
            k_st = jnp.concatenate([kf_st[g], kb_st[g]], axis=1).astype(BF16)
            v_t = v[g * CHUNK:(g + 1) * CHUNK, :].T.astype(BF16)
            kv_ref[c] = _dot(v_t, k_st)
            d_ref[c] = jnp.concatenate([d_f[g], d_b[g]], axis=1)
        return carry

    lax.fori_loop(0, n_groups, pass0, 0)

    def pass1(i, carry):
        st_f, st_b = carry
        cf = i
        cb = n_chunks - 1 - i
        sall_ref[cf, :, 0:LANES] = st_f.astype(BF16)
        sall_ref[cb, :, LANES:2 * LANES] = st_b.astype(BF16)
        st_f = d_ref[cf, :, 0:LANES] * st_f + kv_ref[cf, :, 0:LANES]
        st_b = d_ref[cb, :, LANES:2 * LANES] * st_b + kv_ref[cb, :, LANES:2 * LANES]
        return st_f, st_b

    zero = jnp.zeros((LANES, LANES), F32)
    lax.fori_loop(0, n_chunks, pass1, (zero, zero), unroll=2)

    gain = gain_ref[...]

    rows_o = out_group * CHUNK

    def pass2(gi, carry):
        r0 = pl.multiple_of(gi * rows_o, rows_o)
        rows = pl.ds(r0, rows_o)
        outs = []
        for g in range(out_group):
            c = gi * out_group + g
            rc = pl.ds(pl.multiple_of(r0 + g * CHUNK, CHUNK), CHUNK)
            o = _dot(p_ref[c], v_ref[rc, :]) + _dot_nt(qst_ref[rc, :], sall_ref[c])
            outs.append(o)
        o = jnp.concatenate(outs, axis=0)
        y = o * lax.rsqrt(jnp.mean(o * o, axis=-1, keepdims=True) + RMS_EPS) * gain
        o_ref[rows, :] = (y * gt_ref[rows, :].astype(F32)).astype(BF16)
        return carry

    lax.fori_loop(0, n_chunks // out_group, pass2, 0)


def _block_tri(n, upper):
    r = lax.broadcasted_iota(jnp.int32, (n, n), 0)
    c = lax.broadcasted_iota(jnp.int32, (n, n), 1)
    same = (r // CHUNK) == (c // CHUNK)
    tri = (r <= c) if upper else (r >= c)
    return (same & tri).astype(BF16)


def _gla(proj, lower_bounds, hgrn_g_l, *, layer, n_heads, group):
    bsz, _, s, _ = proj.shape
    depth = lower_bounds.shape[0]
    n_chunks = s // CHUNK
    rows_g = group * CHUNK

    def slab(j):
        return pl.BlockSpec((None, None, s, LANES), lambda b, h: (b, j * n_heads + h, 0, 0))

    out_group = GLA_OUT_GROUP if n_chunks % GLA_OUT_GROUP == 0 else group
    kern = functools.partial(_gla_kernel, layer=layer, group=group, out_group=out_group)
    return pl.pallas_call(
        kern,
        grid=(bsz, n_heads),
        in_specs=[
            slab(0), slab(1), slab(2), slab(3), slab(4),
            pl.BlockSpec((depth, 2, LANES), lambda b, h: (0, 0, h)),
            pl.BlockSpec((1, LANES), lambda b, h: (0, h)),
            pl.BlockSpec((rows_g, rows_g), lambda b, h: (0, 0)),
            pl.BlockSpec((rows_g, rows_g), lambda b, h: (0, 0)),
        ],
        out_specs=pl.BlockSpec((None, None, s, LANES), lambda b, h: (b, h, 0, 0)),
        out_shape=jax.ShapeDtypeStruct((bsz, n_heads, s, LANES), BF16),
        scratch_shapes=[
            pltpu.VMEM((n_chunks, CHUNK, CHUNK), BF16),
            pltpu.VMEM((n_chunks, LANES, 2 * LANES), F32),
            pltpu.VMEM((s, 2 * LANES), BF16),
            pltpu.VMEM((n_chunks, LANES, 2 * LANES), BF16),
            pltpu.VMEM((n_chunks, 1, 2 * LANES), F32),
        ],
        compiler_params=_params("parallel", "parallel"),
        name="gla",
    )(proj, proj, proj, proj, proj, lower_bounds, hgrn_g_l,
      _block_tri(rows_g, False), _block_tri(rows_g, True))


def _out_proj_kernel(yf_ref, yh_ref, x_ref, ada_ref, w_ref, g_ref, b_ref, o_ref, *, mod0, alpha):
    parts = [yf_ref[g] for g in range(yf_ref.shape[0])] + [yh_ref[h] for h in range(yh_ref.shape[0])]
    y = _dot(jnp.concatenate(parts, axis=-1), w_ref[...])
    gate = ada_ref[mod0 + 2:mod0 + 3, :]
    o_ref[...] = _layer_norm(alpha * x_ref[...] + (1.0 + gate) * y, g_ref[...], b_ref[...])


def _out_proj(yf, yh, x, ada_l, w, ln_g, ln_b, *, mod0, alpha, tm):
    bsz, s, d = x.shape
    n_heads = yh.shape[1]
    kern = functools.partial(_out_proj_kernel, mod0=mod0, alpha=alpha)
    return pl.pallas_call(
        kern,
        grid=(bsz, s // tm),
        in_specs=[
            pl.BlockSpec((None, FOURIER_GROUPS, tm, LANES), lambda b, i: (b, 0, i, 0)),
            pl.BlockSpec((None, n_heads, tm, LANES), lambda b, i: (b, 0, i, 0)),
            pl.BlockSpec((None, tm, d), lambda b, i: (b, i, 0)),
            pl.BlockSpec((None, N_MOD, d), lambda b, i: (b, 0, 0)),
            pl.BlockSpec(w.shape, lambda b, i: (0, 0)),
            pl.BlockSpec((1, d), lambda b, i: (0, 0)),
            pl.BlockSpec((1, d), lambda b, i: (0, 0)),
        ],
        out_specs=pl.BlockSpec((None, tm, d), lambda b, i: (b, i, 0)),
        out_shape=jax.ShapeDtypeStruct((bsz, s, d), F32),
        compiler_params=_params("parallel", "parallel"),
        name="out_proj",
    )(yf, yh, x, ada_l, w, ln_g, ln_b)


def _dft_tables(s):
    k = lax.broadcasted_iota(jnp.int32, (s, s), 0)
    n = lax.broadcasted_iota(jnp.int32, (s, s), 1)
    ang = ((k * n) % s).astype(F32) * (2.0 * jnp.pi / s)
    return jnp.cos(ang), jnp.sin(ang)


def _pad_ffn_weights(w_in, w_out):
    d_ff = w_out.shape[0]
    fp = -(-d_ff // FF_ALIGN) * FF_ALIGN
    pad = fp - d_ff
    wa = jnp.pad(w_in[:, :d_ff], ((0, 0), (0, pad)))
    wb = jnp.pad(w_in[:, d_ff:], ((0, 0), (0, pad)))
    return (jnp.concatenate([wa, wb], axis=1).astype(BF16),
            jnp.pad(w_out, ((0, pad), (0, 0))).astype(BF16))


def _reorder_mixer_weights(w, d_hgrn):
    f = D_FOURIER
    q, v, zf, zb, gt = (w[:, f + j * d_hgrn:f + (j + 1) * d_hgrn] for j in range(5))
    return jnp.concatenate([q, gt, v, zf, zb, w[:, :f]], axis=1).astype(BF16)


def kernel(x, c, w_ada, b_ada, w_ffn1_in, w_ffn1_out, ln1_g, ln1_b, w_in, lower_bounds, fourier_g, hgrn_g,
           w_out, ln2_g, ln2_b, w_ffn2_in, w_ffn2_out, ln3_g, ln3_b):
    bsz, s, d = x.shape
    depth = w_ada.shape[0]
    d_hgrn = hgrn_g.shape[1]
    d_in = w_in.shape[2]
    n_heads = d_hgrn // LANES
    assert fourier_g.shape[1] == D_FOURIER and d_in == D_FOURIER + 5 * d_hgrn
    alpha = float((2 * depth) ** 0.25)

    tm = min(s, 1024)
    tm_out = min(s, 512)
    tn_proj = min(d_in, 1024)
    assert (2 * d_hgrn) % tn_proj == 0 and d_in % tn_proj == 0 and tn_proj >= D_FOURIER
    assert s % (GLA_GROUP * CHUNK) == 0
    nb = 2 if bsz % 2 == 0 else 1
    tk_dft = min(2 * s, 2048)

    ada = _ada_proj(c, w_ada, b_ada).reshape(depth, bsz, N_MOD, d)

    cos_s, sin_s = _dft_tables(s)
    dft_mat = jnp.concatenate([cos_s, sin_s], axis=1).astype(BF16)
    cos_c, sin_c = _dft_tables(LANES)
    w_chan = jnp.concatenate([cos_c, -sin_c], axis=1).astype(BF16)
    dft_scale = float((s * LANES) ** -0.5)

    for l in range(depth):
        ada_l = ada[l]
        wi1, wo1 = _pad_ffn_weights(w_ffn1_in[l], w_ffn1_out[l])
        wi2, wo2 = _pad_ffn_weights(w_ffn2_in[l], w_ffn2_out[l])

        x = _ffn(x, ada_l, wi1, wo1, ln1_g[l][None], ln1_b[l][None], mod0=0, alpha=alpha, tm=tm, tf=FF_ALIGN)

        proj, ab = _in_proj(x, ada_l, _reorder_mixer_weights(w_in[l], d_hgrn), w_chan,
                            mod0=3, tm=tm, tn=tn_proj, n_silu=2 * d_hgrn // tn_proj)
        yf = _seq_dft(dft_mat, ab.reshape(2 * s, bsz * D_FOURIER), fourier_g[l][None],
                      bsz=bsz, scale=dft_scale, tm=tm, nb=nb, tk=tk_dft)
        yh = _gla(proj, lower_bounds, hgrn_g[l][None], layer=l, n_heads=n_heads, group=GLA_GROUP)
        x = _out_proj(yf, yh, x, ada_l, w_out[l].astype(BF16), ln2_g[l][None], ln2_b[l][None],
                      mod0=3, alpha=alpha, tm=tm_out)

        x = _ffn(x, ada_l, wi2, wo2, ln3_g[l][None], ln3_b[l][None], mod0=6, alpha=alpha, tm=tm, tf=FF_ALIGN)
    return x
```

```python
import functools

import jax
import jax.numpy as jnp
from jax import lax
from jax.experimental import pallas as pl
from jax.experimental.pallas import tpu as pltpu

F32 = jnp.float32
BF16 = jnp.bfloat16

LN_EPS = 1e-5
RMS_EPS = 1e-6
CHUNK = 64
LANES = 128
N_MOD = 9
FOURIER_GROUPS = 4
D_FOURIER = FOURIER_GROUPS * LANES
FF_ALIGN = 512
FFN_COL_CHUNK = 1024
FFN_ROW_CHUNK = 256
GLA_GROUP = 8
GLA_OUT_GROUP = 16
PROJ_SUB_SLABS = 4
VMEM_LIMIT = 60 * 1024 * 1024


def _params(*sem):
    return pltpu.CompilerParams(dimension_semantics=sem, vmem_limit_bytes=VMEM_LIMIT)


def _sigmoid(z):
    return 1.0 / (1.0 + jnp.exp(-z))


def _silu(z):
    return z * _sigmoid(z)


def _layer_norm(xf, g, b):
    mu = jnp.mean(xf, axis=-1, keepdims=True)
    xc = xf - mu
    var = jnp.mean(xc * xc, axis=-1, keepdims=True)
    return xc * lax.rsqrt(var + LN_EPS) * g + b


def _dot(a, b):
    return jnp.dot(a, b, preferred_element_type=F32)


def _dot_nt(a, b):
    return lax.dot_general(a, b, (((1,), (1,)), ((), ())), preferred_element_type=F32)


def _ada_kernel(c_ref, w_ref, b_ref, o_ref):
    c = c_ref[...]
    o_ref[0] = _dot(_silu(c).astype(BF16), w_ref[0].astype(BF16)) + b_ref[0]


def _ada_proj(c, w_ada, b_ada):
    depth, d, n = w_ada.shape
    bsz = c.shape[0]
    tn = min(d, 1024)
    assert n % tn == 0
    return pl.pallas_call(
        _ada_kernel,
        grid=(depth, n // tn),
        in_specs=[
            pl.BlockSpec((bsz, d), lambda l, j: (0, 0)),
            pl.BlockSpec((1, d, tn), lambda l, j: (l, 0, j)),
            pl.BlockSpec((1, 1, tn), lambda l, j: (l, 0, j)),
        ],
        out_specs=pl.BlockSpec((1, bsz, tn), lambda l, j: (l, 0, j)),
        out_shape=jax.ShapeDtypeStruct((depth, bsz, n), F32),
        compiler_params=_params("parallel", "parallel"),
        name="ada_proj",
    )(c, w_ada, b_ada.reshape(depth, 1, n))


def _ffn_kernel(x_ref, ada_ref, wa_ref, wb_ref, wo_ref, g_ref, b_ref, o_ref, h_ref, *, mod0, alpha):
    j = pl.program_id(2)
    last = pl.num_programs(2) - 1
    tm, d = o_ref.shape

    @pl.when(j == 0)
    def _():
        shift = ada_ref[mod0:mod0 + 1, :]
        scale = ada_ref[mod0 + 1:mod0 + 2, :]
        h_ref[...] = (x_ref[...] * (1.0 + scale) + shift).astype(BF16)
        o_ref[...] = jnp.zeros_like(o_ref)

    h = h_ref[...]
    a = _dot(h, wa_ref[...])
    b = _dot(h, wb_ref[...])
    hidden = (_silu(a) * b).astype(BF16)

    @pl.when(j < last)
    def _():
        cw = min(d, FFN_COL_CHUNK)
        for n in range(0, d, cw):
            o_ref[:, n:n + cw] += _dot(hidden, wo_ref[:, n:n + cw])

    @pl.when(j == last)
    def _():
        coef = 0.5 * (1.0 + ada_ref[mod0 + 2:mod0 + 3, :])
        rc = min(tm, FFN_ROW_CHUNK)
        for r in range(0, tm, rc):
            acc = o_ref[r:r + rc, :] + _dot(hidden[r:r + rc, :], wo_ref[...])
            y = alpha * x_ref[r:r + rc, :] + coef * acc
            o_ref[r:r + rc, :] = _layer_norm(y, g_ref[...], b_ref[...])


def _ffn(x, ada_l, wa, wb, w_out, ln_g, ln_b, *, mod0, alpha, tm, tf):
    bsz, s, d = x.shape
    fp = w_out.shape[0]
    kern = functools.partial(_ffn_kernel, mod0=mod0, alpha=alpha)
    return pl.pallas_call(
        kern,
        grid=(bsz, s // tm, fp // tf),
        in_specs=[
            pl.BlockSpec((None, tm, d), lambda b, i, j: (b, i, 0)),
            pl.BlockSpec((None, N_MOD, d), lambda b, i, j: (b, 0, 0)),
            pl.BlockSpec((d, tf), lambda b, i, j: (0, j)),
            pl.BlockSpec((d, tf), lambda b, i, j: (0, j)),
            pl.BlockSpec((tf, d), lambda b, i, j: (j, 0)),
            pl.BlockSpec((1, d), lambda b, i, j: (0, 0)),
            pl.BlockSpec((1, d), lambda b, i, j: (0, 0)),
        ],
        out_specs=pl.BlockSpec((None, tm, d), lambda b, i, j: (b, i, 0)),
        out_shape=jax.ShapeDtypeStruct((bsz, s, d), F32),
        scratch_shapes=[pltpu.VMEM((tm, d), BF16)],
        compiler_params=_params("parallel", "parallel", "arbitrary"),
        name="ffn",
    )(x, ada_l, wa, wb, w_out, ln_g, ln_b)


def _in_proj_kernel(x_ref, ada_ref, w_ref, wc_ref, o_ref, ab_ref, h_ref, *, mod0, n_silu):
    j = pl.program_id(2)

    @pl.when(j == 0)
    def _():
        shift = ada_ref[mod0:mod0 + 1, :]
        scale = ada_ref[mod0 + 1:mod0 + 2, :]
        h_ref[...] = (x_ref[...] * (1.0 + scale) + shift).astype(BF16)

    is_silu = j < n_silu
    n_slabs = o_ref.shape[0]
    sub = min(n_slabs, PROJ_SUB_SLABS)
    for c0 in range(0, n_slabs, sub):
        res = _dot(h_ref[...], w_ref[:, c0 * LANES:(c0 + sub) * LANES])
        res = jnp.where(is_silu, _silu(res), res)
        for c in range(sub):
            o_ref[c0 + c] = res[:, c * LANES:(c + 1) * LANES].astype(BF16)

    @pl.when(j == pl.num_programs(2) - 1)
    def _():
        for g in range(FOURIER_GROUPS):
            u = o_ref[n_slabs - FOURIER_GROUPS + g]
            cs = _dot(u, wc_ref[...])
            ab_ref[0, :, g * LANES:(g + 1) * LANES] = cs[:, :LANES].astype(BF16)
            ab_ref[1, :, g * LANES:(g + 1) * LANES] = cs[:, LANES:].astype(BF16)


def _in_proj(x, ada_l, w, w_chan, *, mod0, tm, tn, n_silu):
    bsz, s, d = x.shape
    n = w.shape[1]
    kern = functools.partial(_in_proj_kernel, mod0=mod0, n_silu=n_silu)
    return pl.pallas_call(
        kern,
        grid=(bsz, s // tm, n // tn),
        in_specs=[
            pl.BlockSpec((None, tm, d), lambda b, i, j: (b, i, 0)),
            pl.BlockSpec((None, N_MOD, d), lambda b, i, j: (b, 0, 0)),
            pl.BlockSpec((d, tn), lambda b, i, j: (0, j)),
            pl.BlockSpec((LANES, 2 * LANES), lambda b, i, j: (0, 0)),
        ],
        out_specs=[
            pl.BlockSpec((None, tn // LANES, tm, LANES), lambda b, i, j: (b, j, i, 0)),
            pl.BlockSpec((2, tm, D_FOURIER), lambda b, i, j: (0, i, b)),
        ],
        out_shape=[
            jax.ShapeDtypeStruct((bsz, n // LANES, s, LANES), BF16),
            jax.ShapeDtypeStruct((2, s, bsz * D_FOURIER), BF16),
        ],
        scratch_shapes=[pltpu.VMEM((tm, d), BF16)],
        compiler_params=_params("parallel", "parallel", "arbitrary"),
        name="in_proj",
    )(x, ada_l, w, w_chan)


def _seq_dft_kernel(m_ref, ab_ref, g_ref, o_ref, acc_ref, *, scale, nb):
    k = pl.program_id(2)

    @pl.when(k == 0)
    def _():
        acc_ref[...] = jnp.zeros_like(acc_ref)

    acc_ref[...] += _dot(m_ref[...], ab_ref[...])

    @pl.when(k == pl.num_programs(2) - 1)
    def _():
        for bb in range(nb):
            for g in range(FOURIER_GROUPS):
                c0 = (bb * FOURIER_GROUPS + g) * LANES
                y = acc_ref[:, c0:c0 + LANES] * scale
                y = y * lax.rsqrt(jnp.mean(y * y, axis=-1, keepdims=True) + RMS_EPS)
                o_ref[bb, g] = (y * g_ref[:, g * LANES:(g + 1) * LANES]).astype(BF16)


def _seq_dft(dft_mat, ab, gain, *, bsz, scale, tm, nb, tk):
    s = dft_mat.shape[0]
    k_total = dft_mat.shape[1]
    tn = nb * D_FOURIER
    kern = functools.partial(_seq_dft_kernel, scale=scale, nb=nb)
    return pl.pallas_call(
        kern,
        grid=(bsz // nb, s // tm, k_total // tk),
        in_specs=[
            pl.BlockSpec((tm, tk), lambda n, i, k: (i, k)),
            pl.BlockSpec((tk, tn), lambda n, i, k: (k, n)),
            pl.BlockSpec((1, D_FOURIER), lambda n, i, k: (0, 0)),
        ],
        out_specs=pl.BlockSpec((nb, FOURIER_GROUPS, tm, LANES), lambda n, i, k: (n, 0, i, 0)),
        out_shape=jax.ShapeDtypeStruct((bsz, FOURIER_GROUPS, s, LANES), BF16),
        scratch_shapes=[pltpu.VMEM((tm, tn), F32)],
        compiler_params=_params("parallel", "parallel", "arbitrary"),
        name="seq_dft",
    )(dft_mat, ab, gain)


def _split_cumsum(tri, x):
    hi = x.astype(BF16)
    lo = (x - hi.astype(F32)).astype(BF16)
    both = _dot(tri, jnp.concatenate([hi, lo], axis=1))
    return both[:, :LANES] + both[:, LANES:]


def _gla_kernel(q_ref, gt_ref, v_ref, zf_ref, zb_ref, lb_ref, gain_ref, tl_ref, tu_ref, o_ref,
                p_ref, kv_ref, qst_ref, sall_ref, d_ref, *, layer, group, out_group):
    s = q_ref.shape[0]
    n_chunks = s // CHUNK
    rows_g = group * CHUNK
    n_groups = n_chunks // group

    lb_raw = lb_ref[...]
    lb_exp = jnp.exp(lb_raw - jnp.max(lb_raw, axis=0, keepdims=True))
    lb_soft = lb_exp / jnp.sum(lb_exp, axis=0, keepdims=True)
    lbs = jnp.sum(lb_soft[:layer + 1], axis=0) - lb_soft[0]
    lb_f = lbs[0:1, :]
    lb_b = lbs[1:2, :]

    row = lax.broadcasted_iota(jnp.int32, (CHUNK, CHUNK), 0)
    col = lax.broadcasted_iota(jnp.int32, (CHUNK, CHUNK), 1)
    lower = row >= col
    upper = row <= col

    def gate_terms(z_ref, rows, lb):
        z = z_ref[rows, :].astype(F32)
        half_span = 0.5 * (1.0 - lb)
        t = half_span * jnp.tanh(0.5 * z)
        return half_span - t, jnp.log((lb + half_span) + t)

    def scaled(q3, k3, b3, i_ref, i_last):
        b_ref = b3[:, i_ref:i_ref + 1, :]
        b_last = b3[:, i_last:i_last + 1, :]
        q_in = q3 * jnp.exp(b3 - b_ref)
        k_in = k3 * jnp.exp(b_ref - b3)
        q_st = q_in * jnp.exp(b_ref)
        k_st = k_in * jnp.exp(b_last - b_ref)
        return q_in, k_in, q_st, k_st, jnp.exp(b_last)

    def pass0(gi, carry):
        r0 = pl.multiple_of(gi * rows_g, rows_g)
        rows = pl.ds(r0, rows_g)
        q3 = q_ref[rows, :].astype(F32).reshape(group, CHUNK, LANES)
        v = v_ref[rows, :].astype(F32)
        k_f, logf_f = gate_terms(zf_ref, rows, lb_f)
        k_b, logf_b = gate_terms(zb_ref, rows, lb_b)
        b_f = _split_cumsum(tl_ref[...], logf_f)
        b_b = _split_cumsum(tu_ref[...], logf_b)
        shape3 = (group, CHUNK, LANES)
        qf_in, kf_in, qf_st, kf_st, d_f = scaled(q3, k_f.reshape(shape3), b_f.reshape(shape3),
                                                 CHUNK // 2, CHUNK - 1)
        qb_in, kb_in, qb_st, kb_st, d_b = scaled(q3, k_b.reshape(shape3), b_b.reshape(shape3),
                                                 CHUNK - 1 - CHUNK // 2, 0)
        qst = jnp.concatenate([qf_st.reshape(rows_g, LANES), qb_st.reshape(rows_g, LANES)], axis=1)
        qst_ref[rows, :] = qst.astype(BF16)
        for g in range(group):
            c = gi * group + g
            s_f = _dot_nt(qf_in[g].astype(BF16), kf_in[g].astype(BF16))
            s_b = _dot_nt(qb_in[g].astype(BF16), kb_in[g].astype(BF16))
            p_ref[c] = (jnp.where(lower, s_f, 0.0) + jnp.where(upper, s_b, 0.0)).astype(BF16)
            k_st = jnp.concatenate([kf_st[g], kb_st[g]], axis=1).astype(BF16)
            v_t = v[g * CHUNK:(g + 1) * CHUNK, :].T.astype(BF16)
            kv_ref[c] = _dot(v_t, k_st)
            d_ref[c] = jnp.concatenate([d_f[g], d_b[g]], axis=1)
        return carry

    lax.fori_loop(0, n_groups, pass0, 0)

    def pass1(i, carry):
        st_f, st_b = carry
        cf = i
        cb = n_chunks - 1 - i
        sall_ref[cf, :, 0:LANES] = st_f.astype(BF16)
        sall_ref[cb, :, LANES:2 * LANES] = st_b.astype(BF16)
        st_f = d_ref[cf, :, 0:LANES] * st_f + kv_ref[cf, :, 0:LANES]
        st_b = d_ref[cb, :, LANES:2 * LANES] * st_b + kv_ref[cb, :, LANES:2 * LANES]
        return st_f, st_b

    zero = jnp.zeros((LANES, LANES), F32)
    lax.fori_loop(0, n_chunks, pass1, (zero, zero), unroll=2)

    gain = gain_ref[...]

    rows_o = out_group * CHUNK

    def pass2(gi, carry):
        r0 = pl.multiple_of(gi * rows_o, rows_o)
        rows = pl.ds(r0, rows_o)
        outs = []
        for g in range(out_group):
            c = gi * out_group + g
            rc = pl.ds(pl.multiple_of(r0 + g * CHUNK, CHUNK), CHUNK)
            o = _dot(p_ref[c], v_ref[rc, :]) + _dot_nt(qst_ref[rc, :], sall_ref[c])
            outs.append(o)
        o = jnp.concatenate(outs, axis=0)
        y = o * lax.rsqrt(jnp.mean(o * o, axis=-1, keepdims=True) + RMS_EPS) * gain
        o_ref[rows, :] = (y * gt_ref[rows, :].astype(F32)).astype(BF16)
        return carry

    lax.fori_loop(0, n_chunks // out_group, pass2, 0)


def _block_tri(n, upper):
    r = lax.broadcasted_iota(jnp.int32, (n, n), 0)
    c = lax.broadcasted_iota(jnp.int32, (n, n), 1)
    same = (r // CHUNK) == (c // CHUNK)
    tri = (r <= c) if upper else (r >= c)
    return (same & tri).astype(BF16)


def _gla(proj, lower_bounds, hgrn_g_l, *, layer, n_heads, group):
    bsz, _, s, _ = proj.shape
    depth = lower_bounds.shape[0]
    n_chunks = s // CHUNK
    rows_g = group * CHUNK

    def slab(j):
        return pl.BlockSpec((None, None, s, LANES), lambda b, h: (b, j * n_heads + h, 0, 0))

    out_group = GLA_OUT_GROUP if n_chunks % GLA_OUT_GROUP == 0 else group
    kern = functools.partial(_gla_kernel, layer=layer, group=group, out_group=out_group)
    return pl.pallas_call(
        kern,
        grid=(bsz, n_heads),
        in_specs=[
            slab(0), slab(1), slab(2), slab(3), slab(4),
            pl.BlockSpec((depth, 2, LANES), lambda b, h: (0, 0, h)),
            pl.BlockSpec((1, LANES), lambda b, h: (0, h)),
            pl.BlockSpec((rows_g, rows_g), lambda b, h: (0, 0)),
            pl.BlockSpec((rows_g, rows_g), lambda b, h: (0, 0)),
        ],
        out_specs=pl.BlockSpec((None, None, s, LANES), lambda b, h: (b, h, 0, 0)),
        out_shape=jax.ShapeDtypeStruct((bsz, n_heads, s, LANES), BF16),
        scratch_shapes=[
            pltpu.VMEM((n_chunks, CHUNK, CHUNK), BF16),
            pltpu.VMEM((n_chunks, LANES, 2 * LANES), F32),
            pltpu.VMEM((s, 2 * LANES), BF16),
            pltpu.VMEM((n_chunks, LANES, 2 * LANES), BF16),
            pltpu.VMEM((n_chunks, 1, 2 * LANES), F32),
        ],
        compiler_params=_params("parallel", "parallel"),
        name="gla",
    )(proj, proj, proj, proj, proj, lower_bounds, hgrn_g_l,
      _block_tri(rows_g, False), _block_tri(rows_g, True))


def _out_proj_kernel(yf_ref, yh_ref, x_ref, ada_ref, w_ref, g_ref, b_ref, o_ref, *, mod0, alpha):
    coef = 1.0 + ada_ref[mod0 + 2:mod0 + 3, :]
    tm = o_ref.shape[0]
    rc = min(tm, FFN_ROW_CHUNK)
    for r in range(0, tm, rc):
        parts = ([yf_ref[g, r:r + rc, :] for g in range(yf_ref.shape[0])]
                 + [yh_ref[h, r:r + rc, :] for h in range(yh_ref.shape[0])])
        y = _dot(jnp.concatenate(parts, axis=-1), w_ref[...])
        o_ref[r:r + rc, :] = _layer_norm(alpha * x_ref[r:r + rc, :] + coef * y, g_ref[...], b_ref[...])


def _out_proj(yf, yh, x, ada_l, w, ln_g, ln_b, *, mod0, alpha, tm):
    bsz, s, d = x.shape
    n_heads = yh.shape[1]
    kern = functools.partial(_out_proj_kernel, mod0=mod0, alpha=alpha)
    return pl.pallas_call(
        kern,
        grid=(bsz, s // tm),
        in_specs=[
            pl.BlockSpec((None, FOURIER_GROUPS, tm, LANES), lambda b, i: (b, 0, i, 0)),
            pl.BlockSpec((None, n_heads, tm, LANES), lambda b, i: (b, 0, i, 0)),
            pl.BlockSpec((None, tm, d), lambda b, i: (b, i, 0)),
            pl.BlockSpec((None, N_MOD, d), lambda b, i: (b, 0, 0)),
            pl.BlockSpec(w.shape, lambda b, i: (0, 0)),
            pl.BlockSpec((1, d), lambda b, i: (0, 0)),
            pl.BlockSpec((1, d), lambda b, i: (0, 0)),
        ],
        out_specs=pl.BlockSpec((None, tm, d), lambda b, i: (b, i, 0)),
        out_shape=jax.ShapeDtypeStruct((bsz, s, d), F32),
        compiler_params=_params("parallel", "parallel"),
        name="out_proj",
    )(yf, yh, x, ada_l, w, ln_g, ln_b)


def _dft_tables(s):
    k = lax.broadcasted_iota(jnp.int32, (s, s), 0)
    n = lax.broadcasted_iota(jnp.int32, (s, s), 1)
    ang = ((k * n) % s).astype(F32) * (2.0 * jnp.pi / s)
    return jnp.cos(ang), jnp.sin(ang)


def _pad_ffn_weights(w_in, w_out):
    d_ff = w_out.shape[0]
    fp = -(-d_ff // FF_ALIGN) * FF_ALIGN
    pad = fp - d_ff
    wa = jnp.pad(w_in[:, :d_ff].astype(BF16), ((0, 0), (0, pad)))
    wb = jnp.pad(w_in[:, d_ff:].astype(BF16), ((0, 0), (0, pad)))
    return wa, wb, jnp.pad(w_out.astype(BF16), ((0, pad), (0, 0)))


def _reorder_mixer_weights(w, d_hgrn):
    f = D_FOURIER
    q, v, zf, zb, gt = (w[:, f + j * d_hgrn:f + (j + 1) * d_hgrn] for j in range(5))
    return jnp.concatenate([q, gt, v, zf, zb, w[:, :f]], axis=1).astype(BF16)


def kernel(x, c, w_ada, b_ada, w_ffn1_in, w_ffn1_out, ln1_g, ln1_b, w_in, lower_bounds, fourier_g, hgrn_g,
           w_out, ln2_g, ln2_b, w_ffn2_in, w_ffn2_out, ln3_g, ln3_b):
    bsz, s, d = x.shape
    depth = w_ada.shape[0]
    d_hgrn = hgrn_g.shape[1]
    d_in = w_in.shape[2]
    n_heads = d_hgrn // LANES
    assert fourier_g.shape[1] == D_FOURIER and d_in == D_FOURIER + 5 * d_hgrn
    alpha = float((2 * depth) ** 0.25)

    tm = min(s, 1024)
    tm_out = min(s, 512)
    tn_proj = min(d_in, 1024)
    assert (2 * d_hgrn) % tn_proj == 0 and d_in % tn_proj == 0 and tn_proj >= D_FOURIER
    assert s % (GLA_GROUP * CHUNK) == 0
    nb = 2 if bsz % 2 == 0 else 1
    tk_dft = min(2 * s, 2048)

    ada = _ada_proj(c, w_ada, b_ada).reshape(depth, bsz, N_MOD, d)

    cos_s, sin_s = _dft_tables(s)
    dft_mat = jnp.concatenate([cos_s, sin_s], axis=1).astype(BF16)
    cos_c, sin_c = _dft_tables(LANES)
    w_chan = jnp.concatenate([cos_c, -sin_c], axis=1).astype(BF16)
    dft_scale = float((s * LANES) ** -0.5)

    for l in range(depth):
        ada_l = ada[l]
        wa1, wb1, wo1 = _pad_ffn_weights(w_ffn1_in[l], w_ffn1_out[l])
        wa2, wb2, wo2 = _pad_ffn_weights(w_ffn2_in[l], w_ffn2_out[l])

        x = _ffn(x, ada_l, wa1, wb1, wo1, ln1_g[l][None], ln1_b[l][None],
                 mod0=0, alpha=alpha, tm=tm, tf=FF_ALIGN)

        proj, ab = _in_proj(x, ada_l, _reorder_mixer_weights(w_in[l], d_hgrn), w_chan,
                            mod0=3, tm=tm, tn=tn_proj, n_silu=2 * d_hgrn // tn_proj)
        yf = _seq_dft(dft_mat, ab.reshape(2 * s, bsz * D_FOURIER), fourier_g[l][None],
                      bsz=bsz, scale=dft_scale, tm=tm, nb=nb, tk=tk_dft)
        yh = _gla(proj, lower_bounds, hgrn_g[l][None], layer=l, n_heads=n_heads, group=GLA_GROUP)
        x = _out_proj(yf, yh, x, ada_l, w_out[l].astype(BF16), ln2_g[l][None], ln2_b[l][None],
                      mod0=3, alpha=alpha, tm=tm_out)

        x = _ffn(x, ada_l, wa2, wb2, wo2, ln3_g[l][None], ln3_b[l][None],
                 mod0=6, alpha=alpha, tm=tm, tf=FF_ALIGN)
    return x
```

```python
import functools

import jax
import jax.numpy as jnp
from jax import lax
from jax.experimental import pallas as pl
from jax.experimental.pallas import tpu as pltpu

F32 = jnp.float32
BF16 = jnp.bfloat16

LN_EPS = 1e-5
RMS_EPS = 1e-6
LOG2_E = 1.4426950408889634
CHUNK = 64
LANES = 128
N_MOD = 9
FOURIER_GROUPS = 4
D_FOURIER = FOURIER_GROUPS * LANES
FF_ALIGN = 512
FFN_COL_CHUNK = 1024
FFN_ROW_CHUNK = 256
GLA_GROUP = 16
CUMSUM_ROWS = 256
GLA_OUT_GROUP = 16
PROJ_SUB_SLABS = 4
VMEM_LIMIT = 60 * 1024 * 1024


def _params(*sem):
    return pltpu.CompilerParams(dimension_semantics=sem, vmem_limit_bytes=VMEM_LIMIT)


def _sigmoid(z):
    return 1.0 / (1.0 + jnp.exp(-z))


def _silu(z):
    return z * _sigmoid(z)


def _layer_norm(xf, g, b):
    mu = jnp.mean(xf, axis=-1, keepdims=True)
    xc = xf - mu
    var = jnp.mean(xc * xc, axis=-1, keepdims=True)
    return xc * lax.rsqrt(var + LN_EPS) * g + b


def _dot(a, b):
    return jnp.dot(a, b, preferred_element_type=F32)


def _dot_nt(a, b):
    return lax.dot_general(a, b, (((1,), (1,)), ((), ())), preferred_element_type=F32)


def _ada_kernel(c_ref, w_ref, b_ref, o_ref):
    c = c_ref[...]
    o_ref[0] = _dot(_silu(c).astype(BF16), w_ref[0].astype(BF16)) + b_ref[0]


def _ada_proj(c, w_ada, b_ada):
    depth, d, n = w_ada.shape
    bsz = c.shape[0]
    tn = min(d, 1024)
    assert n % tn == 0
    return pl.pallas_call(
        _ada_kernel,
        grid=(depth, n // tn),
        in_specs=[
            pl.BlockSpec((bsz, d), lambda l, j: (0, 0)),
            pl.BlockSpec((1, d, tn), lambda l, j: (l, 0, j)),
            pl.BlockSpec((1, 1, tn), lambda l, j: (l, 0, j)),
        ],
        out_specs=pl.BlockSpec((1, bsz, tn), lambda l, j: (l, 0, j)),
        out_shape=jax.ShapeDtypeStruct((depth, bsz, n), F32),
        compiler_params=_params("parallel", "parallel"),
        name="ada_proj",
    )(c, w_ada, b_ada.reshape(depth, 1, n))


def _ffn_kernel(x_ref, ada_ref, wa_ref, wb_ref, wo_ref, g_ref, b_ref, o_ref, h_ref, hid_ref, *, mod0, alpha):
    j = pl.program_id(2)
    n = pl.num_programs(2) - 1
    tm, d = o_ref.shape

    def up(slot):
        h = h_ref[...]
        a = _dot(h, wa_ref[...])
        b = _dot(h, wb_ref[...])
        hid_ref[slot] = (_silu(a) * b).astype(BF16)

    @pl.when(j == 0)
    def _():
        shift = ada_ref[mod0:mod0 + 1, :]
        scale = ada_ref[mod0 + 1:mod0 + 2, :]
        h_ref[...] = (x_ref[...] * (1.0 + scale) + shift).astype(BF16)
        o_ref[...] = jnp.zeros_like(o_ref)
        up(0)

    @pl.when((j > 0) & (j < n))
    def _():
        prev = hid_ref[(j - 1) % 2]
        up(j % 2)
        cw = min(d, FFN_COL_CHUNK)
        for c in range(0, d, cw):
            o_ref[:, c:c + cw] += _dot(prev, wo_ref[:, c:c + cw])

    @pl.when(j == n)
    def _():
        coef = 0.5 * (1.0 + ada_ref[mod0 + 2:mod0 + 3, :])
        rc = min(tm, FFN_ROW_CHUNK)
        for r in range(0, tm, rc):
            acc = o_ref[r:r + rc, :] + _dot(hid_ref[(j - 1) % 2, r:r + rc, :], wo_ref[...])
            y = alpha * x_ref[r:r + rc, :] + coef * acc
            o_ref[r:r + rc, :] = _layer_norm(y, g_ref[...], b_ref[...])


def _ffn(x, ada_l, wa, wb, w_out, ln_g, ln_b, *, mod0, alpha, tm, tf):
    bsz, s, d = x.shape
    n = w_out.shape[0] // tf
    kern = functools.partial(_ffn_kernel, mod0=mod0, alpha=alpha)
    return pl.pallas_call(
        kern,
        grid=(bsz, s // tm, n + 1),
        in_specs=[
            pl.BlockSpec((None, tm, d), lambda b, i, j: (b, i, 0)),
            pl.BlockSpec((None, N_MOD, d), lambda b, i, j: (b, 0, 0)),
            pl.BlockSpec((d, tf), lambda b, i, j: (0, jnp.minimum(j, n - 1))),
            pl.BlockSpec((d, tf), lambda b, i, j: (0, jnp.minimum(j, n - 1))),
            pl.BlockSpec((tf, d), lambda b, i, j: (jnp.maximum(j - 1, 0), 0)),
            pl.BlockSpec((1, d), lambda b, i, j: (0, 0)),
            pl.BlockSpec((1, d), lambda b, i, j: (0, 0)),
        ],
        out_specs=pl.BlockSpec((None, tm, d), lambda b, i, j: (b, i, 0)),
        out_shape=jax.ShapeDtypeStruct((bsz, s, d), F32),
        scratch_shapes=[pltpu.VMEM((tm, d), BF16), pltpu.VMEM((2, tm, tf), BF16)],
        compiler_params=_params("parallel", "parallel", "arbitrary"),
        name="ffn",
    )(x, ada_l, wa, wb, w_out, ln_g, ln_b)


def _in_proj_kernel(x_ref, ada_ref, w_ref, wc_ref, o_ref, ab_ref, h_ref, *, mod0, n_silu):
    j = pl.program_id(2)

    @pl.when(j == 0)
    def _():
        shift = ada_ref[mod0:mod0 + 1, :]
        scale = ada_ref[mod0 + 1:mod0 + 2, :]
        h_ref[...] = (x_ref[...] * (1.0 + scale) + shift).astype(BF16)

    n_slabs = o_ref.shape[0]
    sub = min(n_slabs, PROJ_SUB_SLABS)
    for c0 in range(0, n_slabs, sub):
        res = _dot(h_ref[...], w_ref[:, c0 * LANES:(c0 + sub) * LANES])
        res = jnp.where(j * n_slabs + c0 < n_silu, _silu(res), res)
        for c in range(sub):
            o_ref[c0 + c] = res[:, c * LANES:(c + 1) * LANES].astype(BF16)

    @pl.when(j == pl.num_programs(2) - 1)
    def _():
        for g in range(FOURIER_GROUPS):
            u = o_ref[n_slabs - FOURIER_GROUPS + g]
            cs = _dot(u, wc_ref[...])
            ab_ref[0, :, g * LANES:(g + 1) * LANES] = cs[:, :LANES].astype(BF16)
            ab_ref[1, :, g * LANES:(g + 1) * LANES] = cs[:, LANES:].astype(BF16)


def _in_proj(x, ada_l, w, w_chan, *, mod0, tm, tn, n_silu):
    bsz, s, d = x.shape
    n = w.shape[1]
    kern = functools.partial(_in_proj_kernel, mod0=mod0, n_silu=n_silu)
    return pl.pallas_call(
        kern,
        grid=(bsz, s // tm, n // tn),
        in_specs=[
            pl.BlockSpec((None, tm, d), lambda b, i, j: (b, i, 0)),
            pl.BlockSpec((None, N_MOD, d), lambda b, i, j: (b, 0, 0)),
            pl.BlockSpec((d, tn), lambda b, i, j: (0, j)),
            pl.BlockSpec((LANES, 2 * LANES), lambda b, i, j: (0, 0)),
        ],
        out_specs=[
            pl.BlockSpec((None, tn // LANES, tm, LANES), lambda b, i, j: (b, j, i, 0)),
            pl.BlockSpec((2, tm, D_FOURIER), lambda b, i, j: (0, i, b)),
        ],
        out_shape=[
            jax.ShapeDtypeStruct((bsz, n // LANES, s, LANES), BF16),
            jax.ShapeDtypeStruct((2, s, bsz * D_FOURIER), BF16),
        ],
        scratch_shapes=[pltpu.VMEM((tm, d), BF16)],
        compiler_params=_params("parallel", "parallel", "arbitrary"),
        name="in_proj",
    )(x, ada_l, w, w_chan)


def _seq_dft_kernel(m_ref, ab_ref, g_ref, o_ref, acc_ref, *, scale, nb):
    k = pl.program_id(2)

    @pl.when(k == 0)
    def _():
        acc_ref[...] = jnp.zeros_like(acc_ref)

    acc_ref[...] += _dot(m_ref[...], ab_ref[...])

    @pl.when(k == pl.num_programs(2) - 1)
    def _():
        for bb in range(nb):
            for g in range(FOURIER_GROUPS):
                c0 = (bb * FOURIER_GROUPS + g) * LANES
                y = acc_ref[:, c0:c0 + LANES] * scale
                y = y * lax.rsqrt(jnp.mean(y * y, axis=-1, keepdims=True) + RMS_EPS)
                o_ref[bb, g] = (y * g_ref[:, g * LANES:(g + 1) * LANES]).astype(BF16)


def _seq_dft(dft_mat, ab, gain, *, bsz, scale, tm, nb, tk):
    s = dft_mat.shape[0]
    k_total = dft_mat.shape[1]
    tn = nb * D_FOURIER
    kern = functools.partial(_seq_dft_kernel, scale=scale, nb=nb)
    return pl.pallas_call(
        kern,
        grid=(bsz // nb, s // tm, k_total // tk),
        in_specs=[
            pl.BlockSpec((tm, tk), lambda n, i, k: (i, k)),
            pl.BlockSpec((tk, tn), lambda n, i, k: (k, n)),
            pl.BlockSpec((1, D_FOURIER), lambda n, i, k: (0, 0)),
        ],
        out_specs=pl.BlockSpec((nb, FOURIER_GROUPS, tm, LANES), lambda n, i, k: (n, 0, i, 0)),
        out_shape=jax.ShapeDtypeStruct((bsz, FOURIER_GROUPS, s, LANES), BF16),
        scratch_shapes=[pltpu.VMEM((tm, tn), F32)],
        compiler_params=_params("parallel", "parallel", "arbitrary"),
        name="seq_dft",
    )(dft_mat, ab, gain)


def _tri_cumsum(tri, x):
    x16 = x.astype(BF16)
    n = tri.shape[0]
    return jnp.concatenate([_dot(tri, x16[r:r + n, :]) for r in range(0, x.shape[0], n)], axis=0)


def _gla_kernel(q_ref, gt_ref, v_ref, zf_ref, zb_ref, lb_ref, gain_ref, tl_ref, tu_ref, o_ref,
                p_ref, kv_ref, qst_ref, sall_ref, d_ref, *, layer, group, out_group):
    s = q_ref.shape[0]
    n_chunks = s // CHUNK
    rows_g = group * CHUNK
    n_groups = n_chunks // group

    lb_raw = lb_ref[...]
    lb_exp = jnp.exp(lb_raw - jnp.max(lb_raw, axis=0, keepdims=True))
    lb_soft = lb_exp / jnp.sum(lb_exp, axis=0, keepdims=True)
    lbs = jnp.sum(lb_soft[:layer + 1], axis=0) - lb_soft[0]
    lb_f = lbs[0:1, :]
    lb_b = lbs[1:2, :]

    row = lax.broadcasted_iota(jnp.int32, (CHUNK, CHUNK), 0)
    col = lax.broadcasted_iota(jnp.int32, (CHUNK, CHUNK), 1)
    lower = row >= col
    upper = row <= col

    def gate_terms(zh_ref, rows, lb):
        half_span = 0.5 * (1.0 - lb)
        t = half_span * jnp.tanh(zh_ref[rows, :].astype(F32))
        return half_span - t, jnp.log((lb + half_span) + t)

    def scaled(q3, k3, b3, i_ref, i_last):
        b_ref = b3[:, i_ref:i_ref + 1, :]
        b_last = b3[:, i_last:i_last + 1, :]
        q_in = q3 * jnp.exp2(b3 - b_ref)
        k_in = k3 * jnp.exp2(b_ref - b3)
        q_st = q_in * jnp.exp2(b_ref)
        k_st = k_in * jnp.exp2(b_last - b_ref)
        return q_in, k_in, q_st, k_st, jnp.exp2(b_last)

    def pass0(gi, carry):
        r0 = pl.multiple_of(gi * rows_g, rows_g)
        rows = pl.ds(r0, rows_g)
        q3 = q_ref[rows, :].astype(F32).reshape(group, CHUNK, LANES)
        v = v_ref[rows, :].astype(F32)
        k_f, logf_f = gate_terms(zf_ref, rows, lb_f)
        k_b, logf_b = gate_terms(zb_ref, rows, lb_b)
        b_f = _tri_cumsum(tl_ref[...], logf_f) * LOG2_E
        b_b = _tri_cumsum(tu_ref[...], logf_b) * LOG2_E
        shape3 = (group, CHUNK, LANES)
        qf_in, kf_in, qf_st, kf_st, d_f = scaled(q3, k_f.reshape(shape3), b_f.reshape(shape3),
                                                 CHUNK // 2, CHUNK - 1)
        qb_in, kb_in, qb_st, kb_st, d_b = scaled(q3, k_b.reshape(shape3), b_b.reshape(shape3),
                                                 CHUNK - 1 - CHUNK // 2, 0)
        qst = jnp.concatenate([qf_st.reshape(rows_g, LANES), qb_st.reshape(rows_g, LANES)], axis=1)
        qst_ref[rows, :] = qst.astype(BF16)
        for g in range(group):
            c = gi * group + g
            s_f = _dot_nt(qf_in[g].astype(BF16), kf_in[g].astype(BF16))
            s_b = _dot_nt(qb_in[g].astype(BF16), kb_in[g].astype(BF16))
            p_ref[c] = (jnp.where(lower, s_f, 0.0) + jnp.where(upper, s_b, 0.0)).astype(BF16)
            k_st = jnp.concatenate([kf_st[g], kb_st[g]], axis=1).astype(BF16)
            v_t = v[g * CHUNK:(g + 1) * CHUNK, :].T.astype(BF16)
            kv_ref[c] = _dot(v_t, k_st)
            d_ref[c] = jnp.concatenate([d_f[g], d_b[g]], axis=1)
        return carry

    lax.fori_loop(0, n_groups, pass0, 0)

    def pass1(i, carry):
        st_f, st_b = carry
        cf = i
        cb = n_chunks - 1 - i
        sall_ref[cf, :, 0:LANES] = st_f.astype(BF16)
        sall_ref[cb, :, LANES:2 * LANES] = st_b.astype(BF16)
        st_f = d_ref[cf, :, 0:LANES] * st_f + kv_ref[cf, :, 0:LANES]
        st_b = d_ref[cb, :, LANES:2 * LANES] * st_b + kv_ref[cb, :, LANES:2 * LANES]
        return st_f, st_b

    zero = jnp.zeros((LANES, LANES), F32)
    lax.fori_loop(0, n_chunks, pass1, (zero, zero), unroll=2)

    gain = gain_ref[...]

    rows_o = out_group * CHUNK

    def pass2(gi, carry):
        r0 = pl.multiple_of(gi * rows_o, rows_o)
        rows = pl.ds(r0, rows_o)
        outs = []
        for g in range(out_group):
            c = gi * out_group + g
            rc = pl.ds(pl.multiple_of(r0 + g * CHUNK, CHUNK), CHUNK)
            o = _dot(p_ref[c], v_ref[rc, :]) + _dot_nt(qst_ref[rc, :], sall_ref[c])
            outs.append(o)
        o = jnp.concatenate(outs, axis=0)
        y = o * lax.rsqrt(jnp.mean(o * o, axis=-1, keepdims=True) + RMS_EPS) * gain
        o_ref[rows, :] = (y * gt_ref[rows, :].astype(F32)).astype(BF16)
        return carry

    lax.fori_loop(0, n_chunks // out_group, pass2, 0)


def _block_tri(n, upper):
    r = lax.broadcasted_iota(jnp.int32, (n, n), 0)
    c = lax.broadcasted_iota(jnp.int32, (n, n), 1)
    same = (r // CHUNK) == (c // CHUNK)
    tri = (r <= c) if upper else (r >= c)
    return (same & tri).astype(BF16)


def _gla(proj, lower_bounds, hgrn_g_l, *, layer, n_heads, group):
    bsz, _, s, _ = proj.shape
    depth = lower_bounds.shape[0]
    n_chunks = s // CHUNK
    rows_t = min(group * CHUNK, CUMSUM_ROWS)

    def slab(j):
        return pl.BlockSpec((None, None, s, LANES), lambda b, h: (b, j * n_heads + h, 0, 0))

    out_group = GLA_OUT_GROUP if n_chunks % GLA_OUT_GROUP == 0 else group
    kern = functools.partial(_gla_kernel, layer=layer, group=group, out_group=out_group)
    return pl.pallas_call(
        kern,
        grid=(bsz, n_heads),
        in_specs=[
            slab(0), slab(1), slab(2), slab(3), slab(4),
            pl.BlockSpec((depth, 2, LANES), lambda b, h: (0, 0, h)),
            pl.BlockSpec((1, LANES), lambda b, h: (0, h)),
            pl.BlockSpec((rows_t, rows_t), lambda b, h: (0, 0)),
            pl.BlockSpec((rows_t, rows_t), lambda b, h: (0, 0)),
        ],
        out_specs=pl.BlockSpec((None, None, s, LANES), lambda b, h: (b, h, 0, 0)),
        out_shape=jax.ShapeDtypeStruct((bsz, n_heads, s, LANES), BF16),
        scratch_shapes=[
            pltpu.VMEM((n_chunks, CHUNK, CHUNK), BF16),
            pltpu.VMEM((n_chunks, LANES, 2 * LANES), F32),
            pltpu.VMEM((s, 2 * LANES), BF16),
            pltpu.VMEM((n_chunks, LANES, 2 * LANES), BF16),
            pltpu.VMEM((n_chunks, 1, 2 * LANES), F32),
        ],
        compiler_params=_params("parallel", "parallel"),
        name="gla",
    )(proj, proj, proj, proj, proj, lower_bounds, hgrn_g_l,
      _block_tri(rows_t, False), _block_tri(rows_t, True))


def _out_proj_kernel(yf_ref, yh_ref, x_ref, ada_ref, w_ref, g_ref, b_ref, o_ref, *, mod0, alpha):
    coef = 1.0 + ada_ref[mod0 + 2:mod0 + 3, :]
    tm = o_ref.shape[0]
    rc = min(tm, FFN_ROW_CHUNK)
    for r in range(0, tm, rc):
        parts = ([yf_ref[g, r:r + rc, :] for g in range(yf_ref.shape[0])]
                 + [yh_ref[h, r:r + rc, :] for h in range(yh_ref.shape[0])])
        y = _dot(jnp.concatenate(parts, axis=-1), w_ref[...])
        o_ref[r:r + rc, :] = _layer_norm(alpha * x_ref[r:r + rc, :] + coef * y, g_ref[...], b_ref[...])


def _out_proj(yf, yh, x, ada_l, w, ln_g, ln_b, *, mod0, alpha, tm):
    bsz, s, d = x.shape
    n_heads = yh.shape[1]
    kern = functools.partial(_out_proj_kernel, mod0=mod0, alpha=alpha)
    return pl.pallas_call(
        kern,
        grid=(bsz, s // tm),
        in_specs=[
            pl.BlockSpec((None, FOURIER_GROUPS, tm, LANES), lambda b, i: (b, 0, i, 0)),
            pl.BlockSpec((None, n_heads, tm, LANES), lambda b, i: (b, 0, i, 0)),
            pl.BlockSpec((None, tm, d), lambda b, i: (b, i, 0)),
            pl.BlockSpec((None, N_MOD, d), lambda b, i: (b, 0, 0)),
            pl.BlockSpec(w.shape, lambda b, i: (0, 0)),
            pl.BlockSpec((1, d), lambda b, i: (0, 0)),
            pl.BlockSpec((1, d), lambda b, i: (0, 0)),
        ],
        out_specs=pl.BlockSpec((None, tm, d), lambda b, i: (b, i, 0)),
        out_shape=jax.ShapeDtypeStruct((bsz, s, d), F32),
        compiler_params=_params("parallel", "parallel"),
        name="out_proj",
    )(yf, yh, x, ada_l, w, ln_g, ln_b)


def _dft_tables(s):
    k = lax.broadcasted_iota(jnp.int32, (s, s), 0)
    n = lax.broadcasted_iota(jnp.int32, (s, s), 1)
    ang = ((k * n) % s).astype(F32) * (2.0 * jnp.pi / s)
    return jnp.cos(ang), jnp.sin(ang)


def _pad_ffn_weights(w_in, w_out):
    d_ff = w_out.shape[0]
    fp = -(-d_ff // FF_ALIGN) * FF_ALIGN
    pad = fp - d_ff
    wa = jnp.pad(w_in[:, :d_ff].astype(BF16), ((0, 0), (0, pad)))
    wb = jnp.pad(w_in[:, d_ff:].astype(BF16), ((0, 0), (0, pad)))
    return wa, wb, jnp.pad(w_out.astype(BF16), ((0, pad), (0, 0)))


def _reorder_mixer_weights(w, d_hgrn):
    f = D_FOURIER
    q, v, zf, zb, gt = (w[:, f + j * d_hgrn:f + (j + 1) * d_hgrn] for j in range(5))
    return jnp.concatenate([q, gt, v, 0.5 * zf, 0.5 * zb, w[:, :f]], axis=1).astype(BF16)


def kernel(x, c, w_ada, b_ada, w_ffn1_in, w_ffn1_out, ln1_g, ln1_b, w_in, lower_bounds, fourier_g, hgrn_g,
           w_out, ln2_g, ln2_b, w_ffn2_in, w_ffn2_out, ln3_g, ln3_b):
    bsz, s, d = x.shape
    depth = w_ada.shape[0]
    d_hgrn = hgrn_g.shape[1]
    d_in = w_in.shape[2]
    n_heads = d_hgrn // LANES
    assert fourier_g.shape[1] == D_FOURIER and d_in == D_FOURIER + 5 * d_hgrn
    alpha = float((2 * depth) ** 0.25)

    tm = min(s, 1024)
    tm_out = min(s, 512)
    tn_proj = next(t for t in (2048, 1024, 512) if d_in % t == 0)
    assert (2 * d_hgrn) % (PROJ_SUB_SLABS * LANES) == 0 and D_FOURIER == PROJ_SUB_SLABS * LANES
    assert s % (GLA_GROUP * CHUNK) == 0
    nb = 2 if bsz % 2 == 0 else 1
    tk_dft = min(2 * s, 2048)

    ada = _ada_proj(c, w_ada, b_ada).reshape(depth, bsz, N_MOD, d)

    cos_s, sin_s = _dft_tables(s)
    dft_mat = jnp.concatenate([cos_s, sin_s], axis=1).astype(BF16)
    cos_c, sin_c = _dft_tables(LANES)
    w_chan = jnp.concatenate([cos_c, -sin_c], axis=1).astype(BF16)
    dft_scale = float((s * LANES) ** -0.5)

    for l in range(depth):
        ada_l = ada[l]
        wa1, wb1, wo1 = _pad_ffn_weights(w_ffn1_in[l], w_ffn1_out[l])
        wa2, wb2, wo2 = _pad_ffn_weights(w_ffn2_in[l], w_ffn2_out[l])

        x = _ffn(x, ada_l, wa1, wb1, wo1, ln1_g[l][None], ln1_b[l][None],
                 mod0=0, alpha=alpha, tm=tm, tf=FF_ALIGN)

        proj, ab = _in_proj(x, ada_l, _reorder_mixer_weights(w_in[l], d_hgrn), w_chan,
                            mod0=3, tm=tm, tn=tn_proj, n_silu=2 * d_hgrn // LANES)
        yf = _seq_dft(dft_mat, ab.reshape(2 * s, bsz * D_FOURIER), fourier_g[l][None],
                      bsz=bsz, scale=dft_scale, tm=tm, nb=nb, tk=tk_dft)
        yh = _gla(proj, lower_bounds, hgrn_g[l][None], layer=l, n_heads=n_heads, group=GLA_GROUP)
        x = _out_proj(yf, yh, x, ada_l, w_out[l].astype(BF16), ln2_g[l][None], ln2_b[l][None],
                      mod0=3, alpha=alpha, tm=tm_out)

        x = _ffn(x, ada_l, wa2, wb2, wo2, ln3_g[l][None], ln3_b[l][None],
                 mod0=6, alpha=alpha, tm=tm, tf=FF_ALIGN)
    return x
```

```python
import functools

import jax
import jax.numpy as jnp
from jax import lax
from jax.experimental import pallas as pl
from jax.experimental.pallas import tpu as pltpu

F32 = jnp.float32
BF16 = jnp.bfloat16

LN_EPS = 1e-5
RMS_EPS = 1e-6
LOG2_E = 1.4426950408889634
CHUNK = 64
LANES = 128
N_MOD = 9
FOURIER_GROUPS = 4
D_FOURIER = FOURIER_GROUPS * LANES
FF_ALIGN = 512
PREP_ROWS = 256
FFN_COL_CHUNK = 1024
FFN_ROW_CHUNK = 256
GLA_GROUP = 16
CUMSUM_ROWS = 256
GLA_OUT_GROUP = 16
PROJ_SUB_SLABS = 4
VMEM_LIMIT = 60 * 1024 * 1024


def _params(*sem):
    return pltpu.CompilerParams(dimension_semantics=sem, vmem_limit_bytes=VMEM_LIMIT)


def _sigmoid(z):
    return 1.0 / (1.0 + jnp.exp(-z))


def _silu(z):
    return z * _sigmoid(z)


def _layer_norm(xf, g, b):
    mu = jnp.mean(xf, axis=-1, keepdims=True)
    xc = xf - mu
    var = jnp.mean(xc * xc, axis=-1, keepdims=True)
    return xc * lax.rsqrt(var + LN_EPS) * g + b


def _dot(a, b):
    return jnp.dot(a, b, preferred_element_type=F32)


def _dot_nt(a, b):
    return lax.dot_general(a, b, (((1,), (1,)), ((), ())), preferred_element_type=F32)


def _ada_kernel(c_ref, w_ref, b_ref, o_ref):
    c = c_ref[...]
    o_ref[0] = _dot(_silu(c).astype(BF16), w_ref[0].astype(BF16)) + b_ref[0]


def _ada_proj(c, w_ada, b_ada):
    depth, d, n = w_ada.shape
    bsz = c.shape[0]
    tn = min(d, 1024)
    assert n % tn == 0
    return pl.pallas_call(
        _ada_kernel,
        grid=(depth, n // tn),
        in_specs=[
            pl.BlockSpec((bsz, d), lambda l, j: (0, 0)),
            pl.BlockSpec((1, d, tn), lambda l, j: (l, 0, j)),
            pl.BlockSpec((1, 1, tn), lambda l, j: (l, 0, j)),
        ],
        out_specs=pl.BlockSpec((1, bsz, tn), lambda l, j: (l, 0, j)),
        out_shape=jax.ShapeDtypeStruct((depth, bsz, n), F32),
        compiler_params=_params("parallel", "parallel"),
        name="ada_proj",
    )(c, w_ada, b_ada.reshape(depth, 1, n))


def _ffn_kernel(x_ref, ada_ref, wa_ref, wb_ref, wo_ref, g_ref, b_ref, o_ref, h_ref, hid_ref, *, mod0, alpha):
    j = pl.program_id(2)
    n = pl.num_programs(2) - 1
    tm, d = o_ref.shape

    def up(slot):
        h = h_ref[...]
        a = _dot(h, wa_ref[...])
        b = _dot(h, wb_ref[...])
        hid_ref[slot] = (_silu(a) * b).astype(BF16)

    @pl.when(j == 0)
    def _():
        shift = ada_ref[mod0:mod0 + 1, :]
        scale = ada_ref[mod0 + 1:mod0 + 2, :]
        h_ref[...] = (x_ref[...] * (1.0 + scale) + shift).astype(BF16)
        o_ref[...] = jnp.zeros_like(o_ref)
        up(0)

    @pl.when((j > 0) & (j < n))
    def _():
        prev = hid_ref[(j - 1) % 2]
        up(j % 2)
        cw = min(d, FFN_COL_CHUNK)
        for c in range(0, d, cw):
            o_ref[:, c:c + cw] += _dot(prev, wo_ref[:, c:c + cw])

    @pl.when(j == n)
    def _():
        coef = 0.5 * (1.0 + ada_ref[mod0 + 2:mod0 + 3, :])
        rc = min(tm, FFN_ROW_CHUNK)
        for r in range(0, tm, rc):
            acc = o_ref[r:r + rc, :] + _dot(hid_ref[(j - 1) % 2, r:r + rc, :], wo_ref[...])
            y = alpha * x_ref[r:r + rc, :] + coef * acc
            o_ref[r:r + rc, :] = _layer_norm(y, g_ref[...], b_ref[...])


def _ffn(x, ada_l, w_ab, w_out, ln_g, ln_b, *, layer, mod0, alpha, tm, tf):
    bsz, s, d = x.shape
    n = w_out.shape[1] // tf
    kern = functools.partial(_ffn_kernel, mod0=mod0, alpha=alpha)
    return pl.pallas_call(
        kern,
        grid=(bsz, s // tm, n + 1),
        in_specs=[
            pl.BlockSpec((None, tm, d), lambda b, i, j: (b, i, 0)),
            pl.BlockSpec((None, N_MOD, d), lambda b, i, j: (b, 0, 0)),
            pl.BlockSpec((None, None, d, tf), lambda b, i, j: (layer, 0, 0, jnp.minimum(j, n - 1))),
            pl.BlockSpec((None, None, d, tf), lambda b, i, j: (layer, 1, 0, jnp.minimum(j, n - 1))),
            pl.BlockSpec((None, tf, d), lambda b, i, j: (layer, jnp.maximum(j - 1, 0), 0)),
            pl.BlockSpec((1, d), lambda b, i, j: (0, 0)),
            pl.BlockSpec((1, d), lambda b, i, j: (0, 0)),
        ],
        out_specs=pl.BlockSpec((None, tm, d), lambda b, i, j: (b, i, 0)),
        out_shape=jax.ShapeDtypeStruct((bsz, s, d), F32),
        scratch_shapes=[pltpu.VMEM((tm, d), BF16), pltpu.VMEM((2, tm, tf), BF16)],
        compiler_params=_params("parallel", "parallel", "arbitrary"),
        name="ffn",
    )(x, ada_l, w_ab, w_ab, w_out, ln_g, ln_b)


def _in_proj_kernel(x_ref, ada_ref, w_ref, wc_ref, o_ref, ab_ref, h_ref, *, mod0, n_heads):
    j = pl.program_id(2)
    q_lo, q_hi, gate_lo = FOURIER_GROUPS, FOURIER_GROUPS + n_heads, FOURIER_GROUPS + 4 * n_heads

    @pl.when(j == 0)
    def _():
        shift = ada_ref[mod0:mod0 + 1, :]
        scale = ada_ref[mod0 + 1:mod0 + 2, :]
        h_ref[...] = (x_ref[...] * (1.0 + scale) + shift).astype(BF16)

    n_slabs = o_ref.shape[0]
    sub = min(n_slabs, PROJ_SUB_SLABS)
    for c0 in range(0, n_slabs, sub):
        res = _dot(h_ref[...], w_ref[:, c0 * LANES:(c0 + sub) * LANES])
        slab = j * n_slabs + c0
        is_silu = ((slab >= q_lo) & (slab < q_hi)) | (slab >= gate_lo)
        res = jnp.where(is_silu, _silu(res), res)
        for c in range(sub):
            o_ref[c0 + c] = res[:, c * LANES:(c + 1) * LANES].astype(BF16)

    @pl.when(j == 0)
    def _():
        for g in range(FOURIER_GROUPS):
            u = o_ref[g]
            cs = _dot(u, wc_ref[...])
            ab_ref[0, :, g * LANES:(g + 1) * LANES] = cs[:, :LANES].astype(BF16)
            ab_ref[1, :, g * LANES:(g + 1) * LANES] = cs[:, LANES:].astype(BF16)


def _in_proj(x, ada_l, w, w_chan, *, layer, mod0, tm, tn, n_heads):
    bsz, s, d = x.shape
    n = w.shape[2]
    assert n_heads % PROJ_SUB_SLABS == 0 and FOURIER_GROUPS == PROJ_SUB_SLABS
    kern = functools.partial(_in_proj_kernel, mod0=mod0, n_heads=n_heads)
    return pl.pallas_call(
        kern,
        grid=(bsz, s // tm, n // tn),
        in_specs=[
            pl.BlockSpec((None, tm, d), lambda b, i, j: (b, i, 0)),
            pl.BlockSpec((None, N_MOD, d), lambda b, i, j: (b, 0, 0)),
            pl.BlockSpec((None, d, tn), lambda b, i, j: (layer, 0, j)),
            pl.BlockSpec((LANES, 2 * LANES), lambda b, i, j: (0, 0)),
        ],
        out_specs=[
            pl.BlockSpec((None, tn // LANES, tm, LANES), lambda b, i, j: (b, j, i, 0)),
            pl.BlockSpec((2, tm, D_FOURIER), lambda b, i, j: (0, i, b)),
        ],
        out_shape=[
            jax.ShapeDtypeStruct((bsz, n // LANES, s, LANES), BF16),
            jax.ShapeDtypeStruct((2, s, bsz * D_FOURIER), BF16),
        ],
        scratch_shapes=[pltpu.VMEM((tm, d), BF16)],
        compiler_params=_params("parallel", "parallel", "arbitrary"),
        name="in_proj",
    )(x, ada_l, w, w_chan)


def _seq_dft_kernel(m_ref, ab_ref, g_ref, o_ref, acc_ref, *, scale, nb):
    k = pl.program_id(2)

    @pl.when(k == 0)
    def _():
        acc_ref[...] = jnp.zeros_like(acc_ref)

    acc_ref[...] += _dot(m_ref[...], ab_ref[...])

    @pl.when(k == pl.num_programs(2) - 1)
    def _():
        for bb in range(nb):
            for g in range(FOURIER_GROUPS):
                c0 = (bb * FOURIER_GROUPS + g) * LANES
                y = acc_ref[:, c0:c0 + LANES] * scale
                y = y * lax.rsqrt(jnp.mean(y * y, axis=-1, keepdims=True) + RMS_EPS)
                o_ref[bb, g] = (y * g_ref[:, g * LANES:(g + 1) * LANES]).astype(BF16)


def _seq_dft(dft_mat, ab, gain, *, bsz, scale, tm, nb, tk):
    s = dft_mat.shape[0]
    k_total = dft_mat.shape[1]
    tn = nb * D_FOURIER
    kern = functools.partial(_seq_dft_kernel, scale=scale, nb=nb)
    return pl.pallas_call(
        kern,
        grid=(bsz // nb, s // tm, k_total // tk),
        in_specs=[
            pl.BlockSpec((tm, tk), lambda n, i, k: (i, k)),
            pl.BlockSpec((tk, tn), lambda n, i, k: (k, n)),
            pl.BlockSpec((1, D_FOURIER), lambda n, i, k: (0, 0)),
        ],
        out_specs=pl.BlockSpec((nb, FOURIER_GROUPS, tm, LANES), lambda n, i, k: (n, 0, i, 0)),
        out_shape=jax.ShapeDtypeStruct((bsz, FOURIER_GROUPS, s, LANES), BF16),
        scratch_shapes=[pltpu.VMEM((tm, tn), F32)],
        compiler_params=_params("parallel", "parallel", "arbitrary"),
        name="seq_dft",
    )(dft_mat, ab, gain)


def _tri_cumsum(tri, x):
    x16 = x.astype(BF16)
    n = tri.shape[0]
    return jnp.concatenate([_dot(tri, x16[r:r + n, :]) for r in range(0, x.shape[0], n)], axis=0)


def _gla_kernel(q_ref, gt_ref, v_ref, zf_ref, zb_ref, lb_ref, gain_ref, tl_ref, tu_ref, o_ref,
                p_ref, kv_ref, qst_ref, sall_ref, d_ref, *, layer, group, out_group):
    s = q_ref.shape[0]
    n_chunks = s // CHUNK
    rows_g = group * CHUNK
    n_groups = n_chunks // group

    lb_raw = lb_ref[...]
    lb_exp = jnp.exp(lb_raw - jnp.max(lb_raw, axis=0, keepdims=True))
    lb_soft = lb_exp / jnp.sum(lb_exp, axis=0, keepdims=True)
    lbs = jnp.sum(lb_soft[:layer + 1], axis=0) - lb_soft[0]
    lb_f = lbs[0:1, :]
    lb_b = lbs[1:2, :]

    row = lax.broadcasted_iota(jnp.int32, (CHUNK, CHUNK), 0)
    col = lax.broadcasted_iota(jnp.int32, (CHUNK, CHUNK), 1)
    lower = row >= col
    upper = row <= col

    def gate_terms(zh_ref, rows, lb):
        half_span = 0.5 * (1.0 - lb)
        t = half_span * jnp.tanh(zh_ref[rows, :].astype(F32))
        return half_span - t, jnp.log((lb + half_span) + t)

    def scaled(q3, k3, b3, i_ref, i_last):
        b_ref = b3[:, i_ref:i_ref + 1, :]
        b_last = b3[:, i_last:i_last + 1, :]
        q_in = q3 * jnp.exp2(b3 - b_ref)
        k_in = k3 * jnp.exp2(b_ref - b3)
        q_st = q_in * jnp.exp2(b_ref)
        k_st = k_in * jnp.exp2(b_last - b_ref)
        return q_in, k_in, q_st, k_st, jnp.exp2(b_last)

    def pass0(gi, carry):
        r0 = pl.multiple_of(gi * rows_g, rows_g)
        rows = pl.ds(r0, rows_g)
        q3 = q_ref[rows, :].astype(F32).reshape(group, CHUNK, LANES)
        v = v_ref[rows, :].astype(F32)
        k_f, logf_f = gate_terms(zf_ref, rows, lb_f)
        k_b, logf_b = gate_terms(zb_ref, rows, lb_b)
        b_f = _tri_cumsum(tl_ref[...], logf_f) * LOG2_E
        b_b = _tri_cumsum(tu_ref[...], logf_b) * LOG2_E
        shape3 = (group, CHUNK, LANES)
        qf_in, kf_in, qf_st, kf_st, d_f = scaled(q3, k_f.reshape(shape3), b_f.reshape(shape3),
                                                 CHUNK // 2, CHUNK - 1)
        qb_in, kb_in, qb_st, kb_st, d_b = scaled(q3, k_b.reshape(shape3), b_b.reshape(shape3),
                                                 CHUNK - 1 - CHUNK // 2, 0)
        qst = jnp.concatenate([qf_st.reshape(rows_g, LANES), qb_st.reshape(rows_g, LANES)], axis=1)
        qst_ref[rows, :] = qst.astype(BF16)
        for g in range(group):
            c = gi * group + g
            s_f = _dot_nt(qf_in[g].astype(BF16), kf_in[g].astype(BF16))
            s_b = _dot_nt(qb_in[g].astype(BF16), kb_in[g].astype(BF16))
            p_ref[c] = (jnp.where(lower, s_f, 0.0) + jnp.where(upper, s_b, 0.0)).astype(BF16)
            k_st = jnp.concatenate([kf_st[g], kb_st[g]], axis=1).astype(BF16)
            v_t = v[g * CHUNK:(g + 1) * CHUNK, :].T.astype(BF16)
            kv_ref[c] = _dot(v_t, k_st)
            d_ref[c] = jnp.concatenate([d_f[g], d_b[g]], axis=1)
        return carry

    lax.fori_loop(0, n_groups, pass0, 0)

    def pass1(i, carry):
        st_f, st_b = carry
        cf = i
        cb = n_chunks - 1 - i
        sall_ref[cf, :, 0:LANES] = st_f.astype(BF16)
        sall_ref[cb, :, LANES:2 * LANES] = st_b.astype(BF16)
        st_f = d_ref[cf, :, 0:LANES] * st_f + kv_ref[cf, :, 0:LANES]
        st_b = d_ref[cb, :, LANES:2 * LANES] * st_b + kv_ref[cb, :, LANES:2 * LANES]
        return st_f, st_b

    zero = jnp.zeros((LANES, LANES), F32)
    lax.fori_loop(0, n_chunks, pass1, (zero, zero), unroll=2)

    gain = gain_ref[...]

    rows_o = out_group * CHUNK

    def pass2(gi, carry):
        r0 = pl.multiple_of(gi * rows_o, rows_o)
        rows = pl.ds(r0, rows_o)
        outs = []
        for g in range(out_group):
            c = gi * out_group + g
            rc = pl.ds(pl.multiple_of(r0 + g * CHUNK, CHUNK), CHUNK)
            o = _dot(p_ref[c], v_ref[rc, :]) + _dot_nt(qst_ref[rc, :], sall_ref[c])
            outs.append(o)
        o = jnp.concatenate(outs, axis=0)
        y = o * lax.rsqrt(jnp.mean(o * o, axis=-1, keepdims=True) + RMS_EPS) * gain
        o_ref[rows, :] = (y * gt_ref[rows, :].astype(F32)).astype(BF16)
        return carry

    lax.fori_loop(0, n_chunks // out_group, pass2, 0)


def _block_tri(n, upper):
    r = lax.broadcasted_iota(jnp.int32, (n, n), 0)
    c = lax.broadcasted_iota(jnp.int32, (n, n), 1)
    same = (r // CHUNK) == (c // CHUNK)
    tri = (r <= c) if upper else (r >= c)
    return (same & tri).astype(BF16)


def _gla(proj, lower_bounds, hgrn_g_l, *, layer, n_heads, group):
    bsz, _, s, _ = proj.shape
    depth = lower_bounds.shape[0]
    n_chunks = s // CHUNK
    rows_t = min(group * CHUNK, CUMSUM_ROWS)

    def slab(j):
        return pl.BlockSpec((None, None, s, LANES), lambda b, h: (b, FOURIER_GROUPS + j * n_heads + h, 0, 0))

    out_group = GLA_OUT_GROUP if n_chunks % GLA_OUT_GROUP == 0 else group
    kern = functools.partial(_gla_kernel, layer=layer, group=group, out_group=out_group)
    return pl.pallas_call(
        kern,
        grid=(bsz, n_heads),
        in_specs=[
            slab(0), slab(4), slab(1), slab(2), slab(3),
            pl.BlockSpec((depth, 2, LANES), lambda b, h: (0, 0, h)),
            pl.BlockSpec((1, LANES), lambda b, h: (0, h)),
            pl.BlockSpec((rows_t, rows_t), lambda b, h: (0, 0)),
            pl.BlockSpec((rows_t, rows_t), lambda b, h: (0, 0)),
        ],
        out_specs=pl.BlockSpec((None, None, s, LANES), lambda b, h: (b, h, 0, 0)),
        out_shape=jax.ShapeDtypeStruct((bsz, n_heads, s, LANES), BF16),
        scratch_shapes=[
            pltpu.VMEM((n_chunks, CHUNK, CHUNK), BF16),
            pltpu.VMEM((n_chunks, LANES, 2 * LANES), F32),
            pltpu.VMEM((s, 2 * LANES), BF16),
            pltpu.VMEM((n_chunks, LANES, 2 * LANES), BF16),
            pltpu.VMEM((n_chunks, 1, 2 * LANES), F32),
        ],
        compiler_params=_params("parallel", "parallel"),
        name="gla",
    )(proj, proj, proj, proj, proj, lower_bounds, hgrn_g_l,
      _block_tri(rows_t, False), _block_tri(rows_t, True))


def _out_proj_kernel(yf_ref, yh_ref, x_ref, ada_ref, w_ref, g_ref, b_ref, o_ref, *, mod0, alpha):
    coef = 1.0 + ada_ref[mod0 + 2:mod0 + 3, :]
    tm = o_ref.shape[0]
    rc = min(tm, FFN_ROW_CHUNK)
    for r in range(0, tm, rc):
        parts = ([yf_ref[g, r:r + rc, :] for g in range(yf_ref.shape[0])]
                 + [yh_ref[h, r:r + rc, :] for h in range(yh_ref.shape[0])])
        y = _dot(jnp.concatenate(parts, axis=-1), w_ref[...])
        o_ref[r:r + rc, :] = _layer_norm(alpha * x_ref[r:r + rc, :] + coef * y, g_ref[...], b_ref[...])


def _out_proj(yf, yh, x, ada_l, w, ln_g, ln_b, *, layer, mod0, alpha, tm):
    bsz, s, d = x.shape
    n_heads = yh.shape[1]
    kern = functools.partial(_out_proj_kernel, mod0=mod0, alpha=alpha)
    return pl.pallas_call(
        kern,
        grid=(bsz, s // tm),
        in_specs=[
            pl.BlockSpec((None, FOURIER_GROUPS, tm, LANES), lambda b, i: (b, 0, i, 0)),
            pl.BlockSpec((None, n_heads, tm, LANES), lambda b, i: (b, 0, i, 0)),
            pl.BlockSpec((None, tm, d), lambda b, i: (b, i, 0)),
            pl.BlockSpec((None, N_MOD, d), lambda b, i: (b, 0, 0)),
            pl.BlockSpec((None,) + w.shape[1:], lambda b, i: (layer, 0, 0)),
            pl.BlockSpec((1, d), lambda b, i: (0, 0)),
            pl.BlockSpec((1, d), lambda b, i: (0, 0)),
        ],
        out_specs=pl.BlockSpec((None, tm, d), lambda b, i: (b, i, 0)),
        out_shape=jax.ShapeDtypeStruct((bsz, s, d), F32),
        compiler_params=_params("parallel", "parallel"),
        name="out_proj",
    )(yf, yh, x, ada_l, w, ln_g, ln_b)


def _dft_tables(s):
    k = lax.broadcasted_iota(jnp.int32, (s, s), 0)
    n = lax.broadcasted_iota(jnp.int32, (s, s), 1)
    ang = ((k * n) % s).astype(F32) * (2.0 * jnp.pi / s)
    return jnp.cos(ang), jnp.sin(ang)


def _cast_pad_cols_kernel(w_ref, o_ref):
    n = w_ref.shape[-1]
    o_ref[:, :n] = w_ref[...].astype(BF16)
    if o_ref.shape[-1] > n:
        o_ref[:, n:] = jnp.zeros((o_ref.shape[0], o_ref.shape[-1] - n), BF16)


def _cast_pad_rows_kernel(w_ref, o_ref):
    n = w_ref.shape[0]
    o_ref[:n, :] = w_ref[...].astype(BF16)
    if o_ref.shape[0] > n:
        o_ref[n:, :] = jnp.zeros((o_ref.shape[0] - n, o_ref.shape[1]), BF16)


def _prep_ffn_weights(w_in, w_out):
    depth, d, two_ff = w_in.shape
    d_ff = two_ff // 2
    fp = -(-d_ff // FF_ALIGN) * FF_ALIGN
    assert d_ff % LANES == 0
    tr = min(d, PREP_ROWS)
    w_ab = pl.pallas_call(
        _cast_pad_cols_kernel,
        grid=(depth, 2, d // tr),
        in_specs=[pl.BlockSpec((None, tr, d_ff), lambda l, h, i: (l, i, h))],
        out_specs=pl.BlockSpec((None, None, tr, fp), lambda l, h, i: (l, h, i, 0)),
        out_shape=jax.ShapeDtypeStruct((depth, 2, d, fp), BF16),
        compiler_params=_params("parallel", "parallel", "parallel"),
        name="prep_w_in",
    )(w_in)
    tc = min(d, PREP_ROWS)
    w_o = pl.pallas_call(
        _cast_pad_rows_kernel,
        grid=(depth, d // tc),
        in_specs=[pl.BlockSpec((None, d_ff, tc), lambda l, i: (l, 0, i))],
        out_specs=pl.BlockSpec((None, fp, tc), lambda l, i: (l, 0, i)),
        out_shape=jax.ShapeDtypeStruct((depth, fp, d), BF16),
        compiler_params=_params("parallel", "parallel"),
        name="prep_w_out",
    )(w_out)
    return w_ab, w_o


def _mixer_weights(w, d_hgrn):
    col = lax.broadcasted_iota(jnp.int32, (1, 1, w.shape[2]), 2)
    lo = D_FOURIER + 2 * d_hgrn
    scale = jnp.where((col >= lo) & (col < lo + 2 * d_hgrn), 0.5, 1.0)
    return (w * scale).astype(BF16)


def kernel(x, c, w_ada, b_ada, w_ffn1_in, w_ffn1_out, ln1_g, ln1_b, w_in, lower_bounds, fourier_g, hgrn_g,
           w_out, ln2_g, ln2_b, w_ffn2_in, w_ffn2_out, ln3_g, ln3_b):
    bsz, s, d = x.shape
    depth = w_ada.shape[0]
    d_hgrn = hgrn_g.shape[1]
    d_in = w_in.shape[2]
    n_heads = d_hgrn // LANES
    assert fourier_g.shape[1] == D_FOURIER and d_in == D_FOURIER + 5 * d_hgrn
    alpha = float((2 * depth) ** 0.25)

    tm = min(s, 1024)
    tm_out = min(s, 512)
    tn_proj = next(t for t in (2048, 1024, 512) if d_in % t == 0)
    assert (2 * d_hgrn) % (PROJ_SUB_SLABS * LANES) == 0 and D_FOURIER == PROJ_SUB_SLABS * LANES
    assert s % (GLA_GROUP * CHUNK) == 0
    nb = 2 if bsz % 2 == 0 else 1
    tk_dft = min(2 * s, 2048)

    ada = _ada_proj(c, w_ada, b_ada).reshape(depth, bsz, N_MOD, d)

    cos_s, sin_s = _dft_tables(s)
    dft_mat = jnp.concatenate([cos_s, sin_s], axis=1).astype(BF16)
    cos_c, sin_c = _dft_tables(LANES)
    w_chan = jnp.concatenate([cos_c, -sin_c], axis=1).astype(BF16)
    dft_scale = float((s * LANES) ** -0.5)

    w_ab1, w_o1 = _prep_ffn_weights(w_ffn1_in, w_ffn1_out)
    w_ab2, w_o2 = _prep_ffn_weights(w_ffn2_in, w_ffn2_out)
    w_mix_in = _mixer_weights(w_in, d_hgrn)
    w_mix_out = w_out.astype(BF16)

    for l in range(depth):
        ada_l = ada[l]
        x = _ffn(x, ada_l, w_ab1, w_o1, ln1_g[l][None], ln1_b[l][None],
                 layer=l, mod0=0, alpha=alpha, tm=tm, tf=FF_ALIGN)

        proj, ab = _in_proj(x, ada_l, w_mix_in, w_chan,
                            layer=l, mod0=3, tm=tm, tn=tn_proj, n_heads=n_heads)
        yf = _seq_dft(dft_mat, ab.reshape(2 * s, bsz * D_FOURIER), fourier_g[l][None],
                      bsz=bsz, scale=dft_scale, tm=tm, nb=nb, tk=tk_dft)
        yh = _gla(proj, lower_bounds, hgrn_g[l][None], layer=l, n_heads=n_heads, group=GLA_GROUP)
        x = _out_proj(yf, yh, x, ada_l, w_mix_out, ln2_g[l][None], ln2_b[l][None],
                      layer=l, mod0=3, alpha=alpha, tm=tm_out)

        x = _ffn(x, ada_l, w_ab2, w_o2, ln3_g[l][None], ln3_b[l][None],
                 layer=l, mod0=6, alpha=alpha, tm=tm, tf=FF_ALIGN)
    return x
```

```python
import functools

import jax
import jax.numpy as jnp
from jax import lax
from jax.experimental import pallas as pl
from jax.experimental.pallas import tpu as pltpu

F32 = jnp.float32
BF16 = jnp.bfloat16

LN_EPS = 1e-5
RMS_EPS = 1e-6
LOG2_E = 1.4426950408889634
CHUNK = 64
LANES = 128
N_MOD = 9
FOURIER_GROUPS = 4
D_FOURIER = FOURIER_GROUPS * LANES
FF_ALIGN = 512
PREP_ROWS = 256
FFN_COL_CHUNK = 1024
FFN_ROW_CHUNK = 256
GLA_GROUP = 16
CUMSUM_ROWS = 256
GLA_OUT_GROUP = 16
PROJ_SUB_SLABS = 4
VMEM_LIMIT = 60 * 1024 * 1024


def _params(*sem):
    return pltpu.CompilerParams(dimension_semantics=sem, vmem_limit_bytes=VMEM_LIMIT)


def _sigmoid(z):
    return 1.0 / (1.0 + jnp.exp(-z))


def _silu(z):
    return z * _sigmoid(z)


def _layer_norm(xf, g, b):
    mu = jnp.mean(xf, axis=-1, keepdims=True)
    xc = xf - mu
    var = jnp.mean(xc * xc, axis=-1, keepdims=True)
    return xc * lax.rsqrt(var + LN_EPS) * g + b


def _dot(a, b):
    return jnp.dot(a, b, preferred_element_type=F32)


def _dot_nt(a, b):
    return lax.dot_general(a, b, (((1,), (1,)), ((), ())), preferred_element_type=F32)


def _ada_kernel(c_ref, w_ref, b_ref, o_ref):
    c = c_ref[...]
    o_ref[0] = _dot(_silu(c).astype(BF16), w_ref[0].astype(BF16)) + b_ref[0]


def _ada_proj(c, w_ada, b_ada):
    depth, d, n = w_ada.shape
    bsz = c.shape[0]
    tn = min(d, 1024)
    assert n % tn == 0
    return pl.pallas_call(
        _ada_kernel,
        grid=(depth, n // tn),
        in_specs=[
            pl.BlockSpec((bsz, d), lambda l, j: (0, 0)),
            pl.BlockSpec((1, d, tn), lambda l, j: (l, 0, j)),
            pl.BlockSpec((1, 1, tn), lambda l, j: (l, 0, j)),
        ],
        out_specs=pl.BlockSpec((1, bsz, tn), lambda l, j: (l, 0, j)),
        out_shape=jax.ShapeDtypeStruct((depth, bsz, n), F32),
        compiler_params=_params("parallel", "parallel"),
        name="ada_proj",
    )(c, w_ada, b_ada.reshape(depth, 1, n))


def _ffn_kernel(x_ref, ada_ref, wab_ref, wo_ref, g_ref, b_ref, o_ref, h_ref, hid_ref, *, mod0, alpha):
    j = pl.program_id(2)
    n = pl.num_programs(2) - 1
    tm, d = o_ref.shape

    def up(slot):
        h = h_ref[...]
        a = _dot(h, wab_ref[0])
        b = _dot(h, wab_ref[1])
        hid_ref[slot] = (_silu(a) * b).astype(BF16)

    @pl.when(j == 0)
    def _():
        shift = ada_ref[mod0:mod0 + 1, :]
        scale = ada_ref[mod0 + 1:mod0 + 2, :]
        h_ref[...] = (x_ref[...] * (1.0 + scale) + shift).astype(BF16)
        o_ref[...] = jnp.zeros_like(o_ref)
        up(0)

    @pl.when((j > 0) & (j < n))
    def _():
        prev = hid_ref[(j - 1) % 2]
        up(j % 2)
        cw = min(d, FFN_COL_CHUNK)
        for c in range(0, d, cw):
            o_ref[:, c:c + cw] += _dot(prev, wo_ref[:, c:c + cw])

    @pl.when(j == n)
    def _():
        coef = 0.5 * (1.0 + ada_ref[mod0 + 2:mod0 + 3, :])
        rc = min(tm, FFN_ROW_CHUNK)
        for r in range(0, tm, rc):
            acc = o_ref[r:r + rc, :] + _dot(hid_ref[(j - 1) % 2, r:r + rc, :], wo_ref[...])
            y = alpha * x_ref[r:r + rc, :] + coef * acc
            o_ref[r:r + rc, :] = _layer_norm(y, g_ref[...], b_ref[...])


def _ffn(x, ada_l, w_ab, w_out, ln_g, ln_b, *, layer, mod0, alpha, tm, tf):
    bsz, s, d = x.shape
    n = w_out.shape[1] // tf
    kern = functools.partial(_ffn_kernel, mod0=mod0, alpha=alpha)
    return pl.pallas_call(
        kern,
        grid=(bsz, s // tm, n + 1),
        in_specs=[
            pl.BlockSpec((None, tm, d), lambda b, i, j: (b, i, 0)),
            pl.BlockSpec((None, N_MOD, d), lambda b, i, j: (b, 0, 0)),
            pl.BlockSpec((None, 2, d, tf), lambda b, i, j: (layer, 0, 0, jnp.minimum(j, n - 1))),
            pl.BlockSpec((None, tf, d), lambda b, i, j: (layer, jnp.maximum(j - 1, 0), 0)),
            pl.BlockSpec((1, d), lambda b, i, j: (0, 0)),
            pl.BlockSpec((1, d), lambda b, i, j: (0, 0)),
        ],
        out_specs=pl.BlockSpec((None, tm, d), lambda b, i, j: (b, i, 0)),
        out_shape=jax.ShapeDtypeStruct((bsz, s, d), F32),
        scratch_shapes=[pltpu.VMEM((tm, d), BF16), pltpu.VMEM((2, tm, tf), BF16)],
        compiler_params=_params("parallel", "parallel", "arbitrary"),
        name="ffn",
    )(x, ada_l, w_ab, w_out, ln_g, ln_b)


def _in_proj_kernel(x_ref, ada_ref, w_ref, wc_ref, o_ref, ab_ref, h_ref, *, mod0, n_heads):
    j = pl.program_id(2)
    q_lo, q_hi, gate_lo = FOURIER_GROUPS, FOURIER_GROUPS + n_heads, FOURIER_GROUPS + 4 * n_heads

    @pl.when(j == 0)
    def _():
        shift = ada_ref[mod0:mod0 + 1, :]
        scale = ada_ref[mod0 + 1:mod0 + 2, :]
        h_ref[...] = (x_ref[...] * (1.0 + scale) + shift).astype(BF16)

    n_slabs = o_ref.shape[0]
    sub = min(n_slabs, PROJ_SUB_SLABS)
    for c0 in range(0, n_slabs, sub):
        res = _dot(h_ref[...], w_ref[:, c0 * LANES:(c0 + sub) * LANES])
        slab = j * n_slabs + c0
        is_silu = ((slab >= q_lo) & (slab < q_hi)) | (slab >= gate_lo)
        res = jnp.where(is_silu, _silu(res), res)
        for c in range(sub):
            o_ref[c0 + c] = res[:, c * LANES:(c + 1) * LANES].astype(BF16)

    @pl.when(j == 0)
    def _():
        for g in range(FOURIER_GROUPS):
            u = o_ref[g]
            cs = _dot(u, wc_ref[...])
            ab_ref[0, :, g * LANES:(g + 1) * LANES] = cs[:, :LANES].astype(BF16)
            ab_ref[1, :, g * LANES:(g + 1) * LANES] = cs[:, LANES:].astype(BF16)


def _in_proj(x, ada_l, w, w_chan, *, layer, mod0, tm, tn, n_heads):
    bsz, s, d = x.shape
    n = w.shape[2]
    assert n_heads % PROJ_SUB_SLABS == 0 and FOURIER_GROUPS == PROJ_SUB_SLABS
    kern = functools.partial(_in_proj_kernel, mod0=mod0, n_heads=n_heads)
    return pl.pallas_call(
        kern,
        grid=(bsz, s // tm, n // tn),
        in_specs=[
            pl.BlockSpec((None, tm, d), lambda b, i, j: (b, i, 0)),
            pl.BlockSpec((None, N_MOD, d), lambda b, i, j: (b, 0, 0)),
            pl.BlockSpec((None, d, tn), lambda b, i, j: (layer, 0, j)),
            pl.BlockSpec((LANES, 2 * LANES), lambda b, i, j: (0, 0)),
        ],
        out_specs=[
            pl.BlockSpec((None, tn // LANES, tm, LANES), lambda b, i, j: (b, j, i, 0)),
            pl.BlockSpec((2, tm, D_FOURIER), lambda b, i, j: (0, i, b)),
        ],
        out_shape=[
            jax.ShapeDtypeStruct((bsz, n // LANES, s, LANES), BF16),
            jax.ShapeDtypeStruct((2, s, bsz * D_FOURIER), BF16),
        ],
        scratch_shapes=[pltpu.VMEM((tm, d), BF16)],
        compiler_params=_params("parallel", "parallel", "arbitrary"),
        name="in_proj",
    )(x, ada_l, w, w_chan)


def _seq_dft_kernel(m_ref, ab_ref, g_ref, o_ref, acc_ref, *, scale, nb):
    k = pl.program_id(2)

    @pl.when(k == 0)
    def _():
        acc_ref[...] = jnp.zeros_like(acc_ref)

    acc_ref[...] += _dot(m_ref[...], ab_ref[...])

    @pl.when(k == pl.num_programs(2) - 1)
    def _():
        for bb in range(nb):
            for g in range(FOURIER_GROUPS):
                c0 = (bb * FOURIER_GROUPS + g) * LANES
                y = acc_ref[:, c0:c0 + LANES] * scale
                y = y * lax.rsqrt(jnp.mean(y * y, axis=-1, keepdims=True) + RMS_EPS)
                o_ref[bb, g] = (y * g_ref[:, g * LANES:(g + 1) * LANES]).astype(BF16)


def _seq_dft(dft_mat, ab, gain, *, bsz, scale, tm, nb, tk):
    s = dft_mat.shape[0]
    k_total = dft_mat.shape[1]
    tn = nb * D_FOURIER
    kern = functools.partial(_seq_dft_kernel, scale=scale, nb=nb)
    return pl.pallas_call(
        kern,
        grid=(bsz // nb, s // tm, k_total // tk),
        in_specs=[
            pl.BlockSpec((tm, tk), lambda n, i, k: (i, k)),
            pl.BlockSpec((tk, tn), lambda n, i, k: (k, n)),
            pl.BlockSpec((1, D_FOURIER), lambda n, i, k: (0, 0)),
        ],
        out_specs=pl.BlockSpec((nb, FOURIER_GROUPS, tm, LANES), lambda n, i, k: (n, 0, i, 0)),
        out_shape=jax.ShapeDtypeStruct((bsz, FOURIER_GROUPS, s, LANES), BF16),
        scratch_shapes=[pltpu.VMEM((tm, tn), F32)],
        compiler_params=_params("parallel", "parallel", "arbitrary"),
        name="seq_dft",
    )(dft_mat, ab, gain)


def _tri_cumsum(tri, x):
    x16 = x.astype(BF16)
    n = tri.shape[0]
    return jnp.concatenate([_dot(tri, x16[r:r + n, :]) for r in range(0, x.shape[0], n)], axis=0)


def _gla_kernel(q_ref, gt_ref, v_ref, zf_ref, zb_ref, lb_ref, gain_ref, tl_ref, tu_ref, o_ref,
                p_ref, kv_ref, qst_ref, sall_ref, d_ref, *, layer, group, out_group):
    s = q_ref.shape[0]
    n_chunks = s // CHUNK
    rows_g = group * CHUNK
    n_groups = n_chunks // group

    lb_raw = lb_ref[...]
    lb_exp = jnp.exp(lb_raw - jnp.max(lb_raw, axis=0, keepdims=True))
    lb_soft = lb_exp / jnp.sum(lb_exp, axis=0, keepdims=True)
    lbs = jnp.sum(lb_soft[:layer + 1], axis=0) - lb_soft[0]
    lb_f = lbs[0:1, :]
    lb_b = lbs[1:2, :]

    row = lax.broadcasted_iota(jnp.int32, (CHUNK, CHUNK), 0)
    col = lax.broadcasted_iota(jnp.int32, (CHUNK, CHUNK), 1)
    lower = row >= col
    upper = row <= col

    def gate_terms(zh_ref, rows, lb):
        half_span = 0.5 * (1.0 - lb)
        t = half_span * jnp.tanh(zh_ref[rows, :].astype(F32))
        return half_span - t, jnp.log((lb + half_span) + t)

    def scaled(q3, k3, b3, i_ref, i_last):
        b_ref = b3[:, i_ref:i_ref + 1, :]
        b_last = b3[:, i_last:i_last + 1, :]
        q_in = q3 * jnp.exp2(b3 - b_ref)
        k_in = k3 * jnp.exp2(b_ref - b3)
        q_st = q_in * jnp.exp2(b_ref)
        k_st = k_in * jnp.exp2(b_last - b_ref)
        return q_in, k_in, q_st, k_st, jnp.exp2(b_last)

    def pass0(gi, carry):
        r0 = pl.multiple_of(gi * rows_g, rows_g)
        rows = pl.ds(r0, rows_g)
        q3 = q_ref[rows, :].astype(F32).reshape(group, CHUNK, LANES)
        v = v_ref[rows, :].astype(F32)
        k_f, logf_f = gate_terms(zf_ref, rows, lb_f)
        k_b, logf_b = gate_terms(zb_ref, rows, lb_b)
        b_f = _tri_cumsum(tl_ref[...], logf_f) * LOG2_E
        b_b = _tri_cumsum(tu_ref[...], logf_b) * LOG2_E
        shape3 = (group, CHUNK, LANES)
        qf_in, kf_in, qf_st, kf_st, d_f = scaled(q3, k_f.reshape(shape3), b_f.reshape(shape3),
                                                 CHUNK // 2, CHUNK - 1)
        qb_in, kb_in, qb_st, kb_st, d_b = scaled(q3, k_b.reshape(shape3), b_b.reshape(shape3),
                                                 CHUNK - 1 - CHUNK // 2, 0)
        qst = jnp.concatenate([qf_st.reshape(rows_g, LANES), qb_st.reshape(rows_g, LANES)], axis=1)
        qst_ref[rows, :] = qst.astype(BF16)
        for g in range(group):
            c = gi * group + g
            s_f = _dot_nt(qf_in[g].astype(BF16), kf_in[g].astype(BF16))
            s_b = _dot_nt(qb_in[g].astype(BF16), kb_in[g].astype(BF16))
            p_ref[c] = (jnp.where(lower, s_f, 0.0) + jnp.where(upper, s_b, 0.0)).astype(BF16)
            k_st = jnp.concatenate([kf_st[g], kb_st[g]], axis=1).astype(BF16)
            v_t = v[g * CHUNK:(g + 1) * CHUNK, :].T.astype(BF16)
            kv_ref[c] = _dot(v_t, k_st)
            d_ref[c] = jnp.concatenate([d_f[g], d_b[g]], axis=1)
        return carry

    lax.fori_loop(0, n_groups, pass0, 0)

    def pass1(i, carry):
        st_f, st_b = carry
        cf = i
        cb = n_chunks - 1 - i
        sall_ref[cf, :, 0:LANES] = st_f.astype(BF16)
        sall_ref[cb, :, LANES:2 * LANES] = st_b.astype(BF16)
        st_f = d_ref[cf, :, 0:LANES] * st_f + kv_ref[cf, :, 0:LANES]
        st_b = d_ref[cb, :, LANES:2 * LANES] * st_b + kv_ref[cb, :, LANES:2 * LANES]
        return st_f, st_b

    zero = jnp.zeros((LANES, LANES), F32)
    lax.fori_loop(0, n_chunks, pass1, (zero, zero), unroll=2)

    gain = gain_ref[...]

    rows_o = out_group * CHUNK

    def pass2(gi, carry):
        r0 = pl.multiple_of(gi * rows_o, rows_o)
        rows = pl.ds(r0, rows_o)
        outs = []
        for g in range(out_group):
            c = gi * out_group + g
            rc = pl.ds(pl.multiple_of(r0 + g * CHUNK, CHUNK), CHUNK)
            o = _dot(p_ref[c], v_ref[rc, :]) + _dot_nt(qst_ref[rc, :], sall_ref[c])
            outs.append(o)
        o = jnp.concatenate(outs, axis=0)
        y = o * lax.rsqrt(jnp.mean(o * o, axis=-1, keepdims=True) + RMS_EPS) * gain
        o_ref[rows, :] = (y * gt_ref[rows, :].astype(F32)).astype(BF16)
        return carry

    lax.fori_loop(0, n_chunks // out_group, pass2, 0)


def _block_tri(n, upper):
    r = lax.broadcasted_iota(jnp.int32, (n, n), 0)
    c = lax.broadcasted_iota(jnp.int32, (n, n), 1)
    same = (r // CHUNK) == (c // CHUNK)
    tri = (r <= c) if upper else (r >= c)
    return (same & tri).astype(BF16)


def _gla(proj, lower_bounds, hgrn_g_l, *, layer, n_heads, group):
    bsz, _, s, _ = proj.shape
    depth = lower_bounds.shape[0]
    n_chunks = s // CHUNK
    rows_t = min(group * CHUNK, CUMSUM_ROWS)

    def slab(j):
        return pl.BlockSpec((None, None, s, LANES), lambda b, h: (b, FOURIER_GROUPS + j * n_heads + h, 0, 0))

    out_group = GLA_OUT_GROUP if n_chunks % GLA_OUT_GROUP == 0 else group
    kern = functools.partial(_gla_kernel, layer=layer, group=group, out_group=out_group)
    return pl.pallas_call(
        kern,
        grid=(bsz, n_heads),
        in_specs=[
            slab(0), slab(4), slab(1), slab(2), slab(3),
            pl.BlockSpec((depth, 2, LANES), lambda b, h: (0, 0, h)),
            pl.BlockSpec((1, LANES), lambda b, h: (0, h)),
            pl.BlockSpec((rows_t, rows_t), lambda b, h: (0, 0)),
            pl.BlockSpec((rows_t, rows_t), lambda b, h: (0, 0)),
        ],
        out_specs=pl.BlockSpec((None, None, s, LANES), lambda b, h: (b, h, 0, 0)),
        out_shape=jax.ShapeDtypeStruct((bsz, n_heads, s, LANES), BF16),
        scratch_shapes=[
            pltpu.VMEM((n_chunks, CHUNK, CHUNK), BF16),
            pltpu.VMEM((n_chunks, LANES, 2 * LANES), F32),
            pltpu.VMEM((s, 2 * LANES), BF16),
            pltpu.VMEM((n_chunks, LANES, 2 * LANES), BF16),
            pltpu.VMEM((n_chunks, 1, 2 * LANES), F32),
        ],
        compiler_params=_params("parallel", "parallel"),
        name="gla",
    )(proj, proj, proj, proj, proj, lower_bounds, hgrn_g_l,
      _block_tri(rows_t, False), _block_tri(rows_t, True))


def _out_proj_kernel(yf_ref, yh_ref, x_ref, ada_ref, w_ref, g_ref, b_ref, o_ref, *, mod0, alpha):
    coef = 1.0 + ada_ref[mod0 + 2:mod0 + 3, :]
    tm = o_ref.shape[0]
    rc = min(tm, FFN_ROW_CHUNK)
    for r in range(0, tm, rc):
        parts = ([yf_ref[g, r:r + rc, :] for g in range(yf_ref.shape[0])]
                 + [yh_ref[h, r:r + rc, :] for h in range(yh_ref.shape[0])])
        y = _dot(jnp.concatenate(parts, axis=-1), w_ref[...])
        o_ref[r:r + rc, :] = _layer_norm(alpha * x_ref[r:r + rc, :] + coef * y, g_ref[...], b_ref[...])


def _out_proj(yf, yh, x, ada_l, w, ln_g, ln_b, *, layer, mod0, alpha, tm):
    bsz, s, d = x.shape
    n_heads = yh.shape[1]
    kern = functools.partial(_out_proj_kernel, mod0=mod0, alpha=alpha)
    return pl.pallas_call(
        kern,
        grid=(bsz, s // tm),
        in_specs=[
            pl.BlockSpec((None, FOURIER_GROUPS, tm, LANES), lambda b, i: (b, 0, i, 0)),
            pl.BlockSpec((None, n_heads, tm, LANES), lambda b, i: (b, 0, i, 0)),
            pl.BlockSpec((None, tm, d), lambda b, i: (b, i, 0)),
            pl.BlockSpec((None, N_MOD, d), lambda b, i: (b, 0, 0)),
            pl.BlockSpec((None,) + w.shape[1:], lambda b, i: (layer, 0, 0)),
            pl.BlockSpec((1, d), lambda b, i: (0, 0)),
            pl.BlockSpec((1, d), lambda b, i: (0, 0)),
        ],
        out_specs=pl.BlockSpec((None, tm, d), lambda b, i: (b, i, 0)),
        out_shape=jax.ShapeDtypeStruct((bsz, s, d), F32),
        compiler_params=_params("parallel", "parallel"),
        name="out_proj",
    )(yf, yh, x, ada_l, w, ln_g, ln_b)


def _dft_tables(s, n_cols, stride=1):
    k = lax.broadcasted_iota(jnp.int32, (s, n_cols), 0)
    n = lax.broadcasted_iota(jnp.int32, (s, n_cols), 1)
    ang = ((k * (n * stride)) % s).astype(F32) * (2.0 * jnp.pi / s)
    return jnp.cos(ang), jnp.sin(ang)


def _seq_dft_matrix_kernel(ac_ref, as_ref, bc_ref, bs_ref, o_ref):
    s = o_ref.shape[1] // 2
    b_c = bc_ref[...]
    b_s = bs_ref[...]
    for n1 in range(s // LANES):
        a_c = ac_ref[:, n1:n1 + 1]
        a_s = as_ref[:, n1:n1 + 1]
        o_ref[:, n1 * LANES:(n1 + 1) * LANES] = (a_c * b_c - a_s * b_s).astype(BF16)
        o_ref[:, s + n1 * LANES:s + (n1 + 1) * LANES] = (a_s * b_c + a_c * b_s).astype(BF16)


def _seq_dft_matrix(s):
    n_coarse = s // LANES
    a_c, a_s = _dft_tables(s, n_coarse, stride=LANES)
    b_c, b_s = _dft_tables(s, LANES)
    tm = min(s, PREP_ROWS)
    coarse = pl.BlockSpec((tm, n_coarse), lambda i: (i, 0))
    fine = pl.BlockSpec((tm, LANES), lambda i: (i, 0))
    return pl.pallas_call(
        _seq_dft_matrix_kernel,
        grid=(s // tm,),
        in_specs=[coarse, coarse, fine, fine],
        out_specs=pl.BlockSpec((tm, 2 * s), lambda i: (i, 0)),
        out_shape=jax.ShapeDtypeStruct((s, 2 * s), BF16),
        compiler_params=_params("parallel"),
        name="seq_dft_matrix",
    )(a_c, a_s, b_c, b_s)


def _cast_pad_cols_kernel(w_ref, o_ref):
    n = w_ref.shape[-1]
    o_ref[:, :n] = w_ref[...].astype(BF16)
    if o_ref.shape[-1] > n:
        o_ref[:, n:] = jnp.zeros((o_ref.shape[0], o_ref.shape[-1] - n), BF16)


def _cast_pad_rows_kernel(w_ref, o_ref):
    n = w_ref.shape[0]
    o_ref[:n, :] = w_ref[...].astype(BF16)
    if o_ref.shape[0] > n:
        o_ref[n:, :] = jnp.zeros((o_ref.shape[0] - n, o_ref.shape[1]), BF16)


def _prep_ffn_weights(w_in, w_out):
    depth, d, two_ff = w_in.shape
    d_ff = two_ff // 2
    fp = -(-d_ff // FF_ALIGN) * FF_ALIGN
    assert d_ff % LANES == 0
    tr = min(d, PREP_ROWS)
    w_ab = pl.pallas_call(
        _cast_pad_cols_kernel,
        grid=(depth, 2, d // tr),
        in_specs=[pl.BlockSpec((None, tr, d_ff), lambda l, h, i: (l, i, h))],
        out_specs=pl.BlockSpec((None, None, tr, fp), lambda l, h, i: (l, h, i, 0)),
        out_shape=jax.ShapeDtypeStruct((depth, 2, d, fp), BF16),
        compiler_params=_params("parallel", "parallel", "parallel"),
        name="prep_w_in",
    )(w_in)
    tc = min(d, PREP_ROWS)
    w_o = pl.pallas_call(
        _cast_pad_rows_kernel,
        grid=(depth, d // tc),
        in_specs=[pl.BlockSpec((None, d_ff, tc), lambda l, i: (l, 0, i))],
        out_specs=pl.BlockSpec((None, fp, tc), lambda l, i: (l, 0, i)),
        out_shape=jax.ShapeDtypeStruct((depth, fp, d), BF16),
        compiler_params=_params("parallel", "parallel"),
        name="prep_w_out",
    )(w_out)
    return w_ab, w_o


def _mixer_weights(w, d_hgrn):
    col = lax.broadcasted_iota(jnp.int32, (1, 1, w.shape[2]), 2)
    lo = D_FOURIER + 2 * d_hgrn
    scale = jnp.where((col >= lo) & (col < lo + 2 * d_hgrn), 0.5, 1.0)
    return (w * scale).astype(BF16)


def kernel(x, c, w_ada, b_ada, w_ffn1_in, w_ffn1_out, ln1_g, ln1_b, w_in, lower_bounds, fourier_g, hgrn_g,
           w_out, ln2_g, ln2_b, w_ffn2_in, w_ffn2_out, ln3_g, ln3_b):
    bsz, s, d = x.shape
    depth = w_ada.shape[0]
    d_hgrn = hgrn_g.shape[1]
    d_in = w_in.shape[2]
    n_heads = d_hgrn // LANES
    assert fourier_g.shape[1] == D_FOURIER and d_in == D_FOURIER + 5 * d_hgrn
    alpha = float((2 * depth) ** 0.25)

    tm = min(s, 1024)
    tm_out = min(s, 512)
    tn_proj = next(t for t in (2048, 1024, 512) if d_in % t == 0)
    assert (2 * d_hgrn) % (PROJ_SUB_SLABS * LANES) == 0 and D_FOURIER == PROJ_SUB_SLABS * LANES
    assert s % (GLA_GROUP * CHUNK) == 0
    nb = 2 if bsz % 2 == 0 else 1
    tk_dft = min(2 * s, 2048)

    ada = _ada_proj(c, w_ada, b_ada).reshape(depth, bsz, N_MOD, d)

    dft_mat = _seq_dft_matrix(s)
    cos_c, sin_c = _dft_tables(LANES, LANES)
    w_chan = jnp.concatenate([cos_c, -sin_c], axis=1).astype(BF16)
    dft_scale = float((s * LANES) ** -0.5)

    w_ab1, w_o1 = _prep_ffn_weights(w_ffn1_in, w_ffn1_out)
    w_ab2, w_o2 = _prep_ffn_weights(w_ffn2_in, w_ffn2_out)
    w_mix_in = _mixer_weights(w_in, d_hgrn)
    w_mix_out = w_out.astype(BF16)

    for l in range(depth):
        ada_l = ada[l]
        x = _ffn(x, ada_l, w_ab1, w_o1, ln1_g[l][None], ln1_b[l][None],
                 layer=l, mod0=0, alpha=alpha, tm=tm, tf=FF_ALIGN)

        proj, ab = _in_proj(x, ada_l, w_mix_in, w_chan,
                            layer=l, mod0=3, tm=tm, tn=tn_proj, n_heads=n_heads)
        yf = _seq_dft(dft_mat, ab.reshape(2 * s, bsz * D_FOURIER), fourier_g[l][None],
                      bsz=bsz, scale=dft_scale, tm=tm, nb=nb, tk=tk_dft)
        yh = _gla(proj, lower_bounds, hgrn_g[l][None], layer=l, n_heads=n_heads, group=GLA_GROUP)
        x = _out_proj(yf, yh, x, ada_l, w_mix_out, ln2_g[l][None], ln2_b[l][None],
                      layer=l, mod0=3, alpha=alpha, tm=tm_out)

        x = _ffn(x, ada_l, w_ab2, w_o2, ln3_g[l][None], ln3_b[l][None],
                 layer=l, mod0=6, alpha=alpha, tm=tm, tf=FF_ALIGN)
    return x
```

```python
import functools

import jax
import jax.numpy as jnp
from jax import lax
from jax.experimental import pallas as pl
from jax.experimental.pallas import tpu as pltpu

F32 = jnp.float32
BF16 = jnp.bfloat16

LN_EPS = 1e-5
RMS_EPS = 1e-6
LOG2_E = 1.4426950408889634
CHUNK = 64
LANES = 128
N_MOD = 9
FOURIER_GROUPS = 4
D_FOURIER = FOURIER_GROUPS * LANES
FF_ALIGN = 512
PREP_ROWS = 256
FFN_COL_CHUNK = 1024
FFN_ROW_CHUNK = 256
GLA_GROUP = 16
CUMSUM_ROWS = 256
GLA_OUT_GROUP = 32
PROJ_SUB_SLABS = 4
VMEM_LIMIT = 60 * 1024 * 1024


def _params(*sem):
    return pltpu.CompilerParams(dimension_semantics=sem, vmem_limit_bytes=VMEM_LIMIT)


def _sigmoid(z):
    return 1.0 / (1.0 + jnp.exp(-z))


def _silu(z):
    return z * (0.5 * jnp.tanh(0.5 * z) + 0.5)


def _layer_norm(xf, g, b):
    mu = jnp.mean(xf, axis=-1, keepdims=True)
    xc = xf - mu
    var = jnp.mean(xc * xc, axis=-1, keepdims=True)
    return xc * lax.rsqrt(var + LN_EPS) * g + b


def _dot(a, b):
    return jnp.dot(a, b, preferred_element_type=F32)


def _dot_nt(a, b):
    return lax.dot_general(a, b, (((1,), (1,)), ((), ())), preferred_element_type=F32)


def _ada_kernel(c_ref, w_ref, b_ref, o_ref):
    c = c_ref[...]
    o_ref[0] = _dot(_silu(c).astype(BF16), w_ref[0].astype(BF16)) + b_ref[0]


def _ada_proj(c, w_ada, b_ada):
    depth, d, n = w_ada.shape
    bsz = c.shape[0]
    tn = min(d, 1024)
    assert n % tn == 0
    return pl.pallas_call(
        _ada_kernel,
        grid=(depth, n // tn),
        in_specs=[
            pl.BlockSpec((bsz, d), lambda l, j: (0, 0)),
            pl.BlockSpec((1, d, tn), lambda l, j: (l, 0, j)),
            pl.BlockSpec((1, 1, tn), lambda l, j: (l, 0, j)),
        ],
        out_specs=pl.BlockSpec((1, bsz, tn), lambda l, j: (l, 0, j)),
        out_shape=jax.ShapeDtypeStruct((depth, bsz, n), F32),
        compiler_params=_params("parallel", "parallel"),
        name="ada_proj",
    )(c, w_ada, b_ada.reshape(depth, 1, n))


def _ffn_kernel(x_ref, ada_ref, wab_ref, wo_ref, g_ref, b_ref, o_ref, h_ref, hid_ref, *, mod0, alpha):
    j = pl.program_id(2)
    n = pl.num_programs(2) - 1
    tm, d = o_ref.shape

    def up(slot):
        h = h_ref[...]
        a = _dot(h, wab_ref[0])
        b = _dot(h, wab_ref[1])
        hid_ref[slot] = (_silu(a) * b).astype(BF16)

    @pl.when(j == 0)
    def _():
        shift = ada_ref[mod0:mod0 + 1, :]
        scale = ada_ref[mod0 + 1:mod0 + 2, :]
        h_ref[...] = (x_ref[...] * (1.0 + scale) + shift).astype(BF16)
        o_ref[...] = jnp.zeros_like(o_ref)
        up(0)

    @pl.when((j > 0) & (j < n))
    def _():
        prev = hid_ref[(j - 1) % 2]
        up(j % 2)
        cw = min(d, FFN_COL_CHUNK)
        for c in range(0, d, cw):
            o_ref[:, c:c + cw] += _dot(prev, wo_ref[:, c:c + cw])

    @pl.when(j == n)
    def _():
        coef = 0.5 * (1.0 + ada_ref[mod0 + 2:mod0 + 3, :])
        rc = min(tm, FFN_ROW_CHUNK)
        for r in range(0, tm, rc):
            acc = o_ref[r:r + rc, :] + _dot(hid_ref[(j - 1) % 2, r:r + rc, :], wo_ref[...])
            y = alpha * x_ref[r:r + rc, :] + coef * acc
            o_ref[r:r + rc, :] = _layer_norm(y, g_ref[...], b_ref[...])


def _ffn(x, ada_l, w_ab, w_out, ln_g, ln_b, *, layer, mod0, alpha, tm, tf):
    bsz, s, d = x.shape
    n = w_out.shape[1] // tf
    kern = functools.partial(_ffn_kernel, mod0=mod0, alpha=alpha)
    return pl.pallas_call(
        kern,
        grid=(bsz, s // tm, n + 1),
        in_specs=[
            pl.BlockSpec((None, tm, d), lambda b, i, j: (b, i, 0)),
            pl.BlockSpec((None, N_MOD, d), lambda b, i, j: (b, 0, 0)),
            pl.BlockSpec((None, 2, d, tf), lambda b, i, j: (layer, 0, 0, jnp.minimum(j, n - 1))),
            pl.BlockSpec((None, tf, d), lambda b, i, j: (layer, jnp.maximum(j - 1, 0), 0)),
            pl.BlockSpec((1, d), lambda b, i, j: (0, 0)),
            pl.BlockSpec((1, d), lambda b, i, j: (0, 0)),
        ],
        out_specs=pl.BlockSpec((None, tm, d), lambda b, i, j: (b, i, 0)),
        out_shape=jax.ShapeDtypeStruct((bsz, s, d), F32),
        scratch_shapes=[pltpu.VMEM((tm, d), BF16), pltpu.VMEM((2, tm, tf), BF16)],
        compiler_params=_params("parallel", "parallel", "arbitrary"),
        name="ffn",
    )(x, ada_l, w_ab, w_out, ln_g, ln_b)


def _in_proj_kernel(x_ref, ada_ref, w_ref, wc_ref, o_ref, ab_ref, h_ref, *, mod0, n_heads):
    j = pl.program_id(2)
    q_lo, q_hi, gate_lo = FOURIER_GROUPS, FOURIER_GROUPS + n_heads, FOURIER_GROUPS + 4 * n_heads

    @pl.when(j == 0)
    def _():
        shift = ada_ref[mod0:mod0 + 1, :]
        scale = ada_ref[mod0 + 1:mod0 + 2, :]
        h_ref[...] = (x_ref[...] * (1.0 + scale) + shift).astype(BF16)

    n_slabs = o_ref.shape[0]
    sub = min(n_slabs, PROJ_SUB_SLABS)
    for c0 in range(0, n_slabs, sub):
        res = _dot(h_ref[...], w_ref[:, c0 * LANES:(c0 + sub) * LANES])
        slab = j * n_slabs + c0
        is_silu = ((slab >= q_lo) & (slab < q_hi)) | (slab >= gate_lo)
        res = jnp.where(is_silu, _silu(res), res)
        for c in range(sub):
            o_ref[c0 + c] = res[:, c * LANES:(c + 1) * LANES].astype(BF16)

    @pl.when(j == 0)
    def _():
        for g in range(FOURIER_GROUPS):
            u = o_ref[g]
            cs = _dot(u, wc_ref[...])
            ab_ref[0, :, g * LANES:(g + 1) * LANES] = cs[:, :LANES].astype(BF16)
            ab_ref[1, :, g * LANES:(g + 1) * LANES] = cs[:, LANES:].astype(BF16)


def _in_proj(x, ada_l, w, w_chan, *, layer, mod0, tm, tn, n_heads):
    bsz, s, d = x.shape
    n = w.shape[2]
    assert n_heads % PROJ_SUB_SLABS == 0 and FOURIER_GROUPS == PROJ_SUB_SLABS
    kern = functools.partial(_in_proj_kernel, mod0=mod0, n_heads=n_heads)
    return pl.pallas_call(
        kern,
        grid=(bsz, s // tm, n // tn),
        in_specs=[
            pl.BlockSpec((None, tm, d), lambda b, i, j: (b, i, 0)),
            pl.BlockSpec((None, N_MOD, d), lambda b, i, j: (b, 0, 0)),
            pl.BlockSpec((None, d, tn), lambda b, i, j: (layer, 0, j)),
            pl.BlockSpec((LANES, 2 * LANES), lambda b, i, j: (0, 0)),
        ],
        out_specs=[
            pl.BlockSpec((None, tn // LANES, tm, LANES), lambda b, i, j: (b, j, i, 0)),
            pl.BlockSpec((2, tm, D_FOURIER), lambda b, i, j: (0, i, b)),
        ],
        out_shape=[
            jax.ShapeDtypeStruct((bsz, n // LANES, s, LANES), BF16),
            jax.ShapeDtypeStruct((2, s, bsz * D_FOURIER), BF16),
        ],
        scratch_shapes=[pltpu.VMEM((tm, d), BF16)],
        compiler_params=_params("parallel", "parallel", "arbitrary"),
        name="in_proj",
    )(x, ada_l, w, w_chan)


def _seq_dft_kernel(m_ref, ab_ref, g_ref, o_ref, acc_ref, *, scale, nb):
    k = pl.program_id(2)

    @pl.when(k == 0)
    def _():
        acc_ref[...] = jnp.zeros_like(acc_ref)

    acc_ref[...] += _dot(m_ref[...], ab_ref[...])

    @pl.when(k == pl.num_programs(2) - 1)
    def _():
        for bb in range(nb):
            for g in range(FOURIER_GROUPS):
                c0 = (bb * FOURIER_GROUPS + g) * LANES
                y = acc_ref[:, c0:c0 + LANES] * scale
                y = y * lax.rsqrt(jnp.mean(y * y, axis=-1, keepdims=True) + RMS_EPS)
                o_ref[bb, g] = (y * g_ref[:, g * LANES:(g + 1) * LANES]).astype(BF16)


def _seq_dft(dft_mat, ab, gain, *, bsz, scale, tm, nb, tk):
    s = dft_mat.shape[0]
    k_total = dft_mat.shape[1]
    tn = nb * D_FOURIER
    kern = functools.partial(_seq_dft_kernel, scale=scale, nb=nb)
    return pl.pallas_call(
        kern,
        grid=(bsz // nb, s // tm, k_total // tk),
        in_specs=[
            pl.BlockSpec((tm, tk), lambda n, i, k: (i, k)),
            pl.BlockSpec((tk, tn), lambda n, i, k: (k, n)),
            pl.BlockSpec((1, D_FOURIER), lambda n, i, k: (0, 0)),
        ],
        out_specs=pl.BlockSpec((nb, FOURIER_GROUPS, tm, LANES), lambda n, i, k: (n, 0, i, 0)),
        out_shape=jax.ShapeDtypeStruct((bsz, FOURIER_GROUPS, s, LANES), BF16),
        scratch_shapes=[pltpu.VMEM((tm, tn), F32)],
        compiler_params=_params("parallel", "parallel", "arbitrary"),
        name="seq_dft",
    )(dft_mat, ab, gain)


def _tri_cumsum(tri, x):
    x16 = x.astype(BF16)
    n = tri.shape[0]
    return jnp.concatenate([_dot(tri, x16[r:r + n, :]) for r in range(0, x.shape[0], n)], axis=0)


def _gla_kernel(q_ref, gt_ref, v_ref, zf_ref, zb_ref, lb_ref, gain_ref, tl_ref, tu_ref, o_ref,
                p_ref, kv_ref, qst_ref, sall_ref, d_ref, *, layer, group, out_group):
    s = q_ref.shape[0]
    n_chunks = s // CHUNK
    rows_g = group * CHUNK
    n_groups = n_chunks // group

    lb_raw = lb_ref[...]
    lb_exp = jnp.exp(lb_raw - jnp.max(lb_raw, axis=0, keepdims=True))
    lb_soft = lb_exp / jnp.sum(lb_exp, axis=0, keepdims=True)
    lbs = jnp.sum(lb_soft[:layer + 1], axis=0) - lb_soft[0]
    lb_f = lbs[0:1, :]
    lb_b = lbs[1:2, :]

    row = lax.broadcasted_iota(jnp.int32, (CHUNK, CHUNK), 0)
    col = lax.broadcasted_iota(jnp.int32, (CHUNK, CHUNK), 1)
    lower = row >= col
    upper = row <= col

    def gate_terms(zh_ref, rows, lb):
        half_span = 0.5 * (1.0 - lb)
        t = half_span * jnp.tanh(zh_ref[rows, :].astype(F32))
        return half_span - t, jnp.log((lb + half_span) + t)

    def scaled(q3, k3, b3, i_ref, i_last):
        b_ref = b3[:, i_ref:i_ref + 1, :]
        b_last = b3[:, i_last:i_last + 1, :]
        q_in = q3 * jnp.exp2(b3 - b_ref)
        k_in = k3 * jnp.exp2(b_ref - b3)
        q_st = q_in * jnp.exp2(b_ref)
        k_st = k_in * jnp.exp2(b_last - b_ref)
        return q_in, k_in, q_st, k_st, jnp.exp2(b_last)

    def pass0(gi, carry):
        r0 = pl.multiple_of(gi * rows_g, rows_g)
        rows = pl.ds(r0, rows_g)
        q3 = q_ref[rows, :].astype(F32).reshape(group, CHUNK, LANES)
        v = v_ref[rows, :].astype(F32)
        k_f, logf_f = gate_terms(zf_ref, rows, lb_f)
        k_b, logf_b = gate_terms(zb_ref, rows, lb_b)
        b_f = _tri_cumsum(tl_ref[...], logf_f) * LOG2_E
        b_b = _tri_cumsum(tu_ref[...], logf_b) * LOG2_E
        shape3 = (group, CHUNK, LANES)
        qf_in, kf_in, qf_st, kf_st, d_f = scaled(q3, k_f.reshape(shape3), b_f.reshape(shape3),
                                                 CHUNK // 2, CHUNK - 1)
        qb_in, kb_in, qb_st, kb_st, d_b = scaled(q3, k_b.reshape(shape3), b_b.reshape(shape3),
                                                 CHUNK - 1 - CHUNK // 2, 0)
        qst = jnp.concatenate([qf_st.reshape(rows_g, LANES), qb_st.reshape(rows_g, LANES)], axis=1)
        qst_ref[rows, :] = qst.astype(BF16)
        for g in range(group):
            c = gi * group + g
            s_f = _dot_nt(qf_in[g].astype(BF16), kf_in[g].astype(BF16))
            s_b = _dot_nt(qb_in[g].astype(BF16), kb_in[g].astype(BF16))
            p_ref[c] = (jnp.where(lower, s_f, 0.0) + jnp.where(upper, s_b, 0.0)).astype(BF16)
            k_st = jnp.concatenate([kf_st[g], kb_st[g]], axis=1).astype(BF16)
            v_t = v[g * CHUNK:(g + 1) * CHUNK, :].T.astype(BF16)
            kv_ref[c] = _dot(v_t, k_st)
            d_ref[c] = jnp.concatenate([d_f[g], d_b[g]], axis=1)
        return carry

    lax.fori_loop(0, n_groups, pass0, 0)

    def pass1(i, carry):
        st_f, st_b = carry
        cf = i
        cb = n_chunks - 1 - i
        sall_ref[cf, :, 0:LANES] = st_f.astype(BF16)
        sall_ref[cb, :, LANES:2 * LANES] = st_b.astype(BF16)
        st_f = d_ref[cf, :, 0:LANES] * st_f + kv_ref[cf, :, 0:LANES]
        st_b = d_ref[cb, :, LANES:2 * LANES] * st_b + kv_ref[cb, :, LANES:2 * LANES]
        return st_f, st_b

    zero = jnp.zeros((LANES, LANES), F32)
    lax.fori_loop(0, n_chunks, pass1, (zero, zero), unroll=2)

    gain = gain_ref[...]

    rows_o = out_group * CHUNK

    def pass2(gi, carry):
        r0 = pl.multiple_of(gi * rows_o, rows_o)
        rows = pl.ds(r0, rows_o)
        outs = []
        for g in range(out_group):
            c = gi * out_group + g
            rc = pl.ds(pl.multiple_of(r0 + g * CHUNK, CHUNK), CHUNK)
            o = _dot(p_ref[c], v_ref[rc, :]) + _dot_nt(qst_ref[rc, :], sall_ref[c])
            outs.append(o)
        o = jnp.concatenate(outs, axis=0)
        y = o * lax.rsqrt(jnp.mean(o * o, axis=-1, keepdims=True) + RMS_EPS) * gain
        o_ref[rows, :] = (y * gt_ref[rows, :].astype(F32)).astype(BF16)
        return carry

    lax.fori_loop(0, n_chunks // out_group, pass2, 0)


def _block_tri(n, upper):
    r = lax.broadcasted_iota(jnp.int32, (n, n), 0)
    c = lax.broadcasted_iota(jnp.int32, (n, n), 1)
    same = (r // CHUNK) == (c // CHUNK)
    tri = (r <= c) if upper else (r >= c)
    return (same & tri).astype(BF16)


def _gla(proj, lower_bounds, hgrn_g_l, *, layer, n_heads, group):
    bsz, _, s, _ = proj.shape
    depth = lower_bounds.shape[0]
    n_chunks = s // CHUNK
    rows_t = min(group * CHUNK, CUMSUM_ROWS)

    def slab(j):
        return pl.BlockSpec((None, None, s, LANES), lambda b, h: (b, FOURIER_GROUPS + j * n_heads + h, 0, 0))

    out_group = GLA_OUT_GROUP if n_chunks % GLA_OUT_GROUP == 0 else group
    kern = functools.partial(_gla_kernel, layer=layer, group=group, out_group=out_group)
    return pl.pallas_call(
        kern,
        grid=(bsz, n_heads),
        in_specs=[
            slab(0), slab(4), slab(1), slab(2), slab(3),
            pl.BlockSpec((depth, 2, LANES), lambda b, h: (0, 0, h)),
            pl.BlockSpec((1, LANES), lambda b, h: (0, h)),
            pl.BlockSpec((rows_t, rows_t), lambda b, h: (0, 0)),
            pl.BlockSpec((rows_t, rows_t), lambda b, h: (0, 0)),
        ],
        out_specs=pl.BlockSpec((None, None, s, LANES), lambda b, h: (b, h, 0, 0)),
        out_shape=jax.ShapeDtypeStruct((bsz, n_heads, s, LANES), BF16),
        scratch_shapes=[
            pltpu.VMEM((n_chunks, CHUNK, CHUNK), BF16),
            pltpu.VMEM((n_chunks, LANES, 2 * LANES), F32),
            pltpu.VMEM((s, 2 * LANES), BF16),
            pltpu.VMEM((n_chunks, LANES, 2 * LANES), BF16),
            pltpu.VMEM((n_chunks, 1, 2 * LANES), F32),
        ],
        compiler_params=_params("parallel", "parallel"),
        name="gla",
    )(proj, proj, proj, proj, proj, lower_bounds, hgrn_g_l,
      _block_tri(rows_t, False), _block_tri(rows_t, True))


def _out_proj_kernel(yf_ref, yh_ref, x_ref, ada_ref, w_ref, g_ref, b_ref, o_ref, *, mod0, alpha):
    coef = 1.0 + ada_ref[mod0 + 2:mod0 + 3, :]
    tm = o_ref.shape[0]
    rc = min(tm, FFN_ROW_CHUNK)
    for r in range(0, tm, rc):
        parts = ([yf_ref[g, r:r + rc, :] for g in range(yf_ref.shape[0])]
                 + [yh_ref[h, r:r + rc, :] for h in range(yh_ref.shape[0])])
        y = _dot(jnp.concatenate(parts, axis=-1), w_ref[...])
        o_ref[r:r + rc, :] = _layer_norm(alpha * x_ref[r:r + rc, :] + coef * y, g_ref[...], b_ref[...])


def _out_proj(yf, yh, x, ada_l, w, ln_g, ln_b, *, layer, mod0, alpha, tm):
    bsz, s, d = x.shape
    n_heads = yh.shape[1]
    kern = functools.partial(_out_proj_kernel, mod0=mod0, alpha=alpha)
    return pl.pallas_call(
        kern,
        grid=(bsz, s // tm),
        in_specs=[
            pl.BlockSpec((None, FOURIER_GROUPS, tm, LANES), lambda b, i: (b, 0, i, 0)),
            pl.BlockSpec((None, n_heads, tm, LANES), lambda b, i: (b, 0, i, 0)),
            pl.BlockSpec((None, tm, d), lambda b, i: (b, i, 0)),
            pl.BlockSpec((None, N_MOD, d), lambda b, i: (b, 0, 0)),
            pl.BlockSpec((None,) + w.shape[1:], lambda b, i: (layer, 0, 0)),
            pl.BlockSpec((1, d), lambda b, i: (0, 0)),
            pl.BlockSpec((1, d), lambda b, i: (0, 0)),
        ],
        out_specs=pl.BlockSpec((None, tm, d), lambda b, i: (b, i, 0)),
        out_shape=jax.ShapeDtypeStruct((bsz, s, d), F32),
        compiler_params=_params("parallel", "parallel"),
        name="out_proj",
    )(yf, yh, x, ada_l, w, ln_g, ln_b)


def _dft_tables(s, n_cols, stride=1):
    k = lax.broadcasted_iota(jnp.int32, (s, n_cols), 0)
    n = lax.broadcasted_iota(jnp.int32, (s, n_cols), 1)
    ang = ((k * (n * stride)) % s).astype(F32) * (2.0 * jnp.pi / s)
    return jnp.cos(ang), jnp.sin(ang)


def _seq_dft_matrix_kernel(ac_ref, as_ref, bc_ref, bs_ref, o_ref):
    s = o_ref.shape[1] // 2
    b_c = bc_ref[...]
    b_s = bs_ref[...]
    for n1 in range(s // LANES):
        a_c = ac_ref[:, n1:n1 + 1]
        a_s = as_ref[:, n1:n1 + 1]
        o_ref[:, n1 * LANES:(n1 + 1) * LANES] = (a_c * b_c - a_s * b_s).astype(BF16)
        o_ref[:, s + n1 * LANES:s + (n1 + 1) * LANES] = (a_s * b_c + a_c * b_s).astype(BF16)


def _seq_dft_matrix(s):
    n_coarse = s // LANES
    a_c, a_s = _dft_tables(s, n_coarse, stride=LANES)
    b_c, b_s = _dft_tables(s, LANES)
    tm = min(s, PREP_ROWS)
    coarse = pl.BlockSpec((tm, n_coarse), lambda i: (i, 0))
    fine = pl.BlockSpec((tm, LANES), lambda i: (i, 0))
    return pl.pallas_call(
        _seq_dft_matrix_kernel,
        grid=(s // tm,),
        in_specs=[coarse, coarse, fine, fine],
        out_specs=pl.BlockSpec((tm, 2 * s), lambda i: (i, 0)),
        out_shape=jax.ShapeDtypeStruct((s, 2 * s), BF16),
        compiler_params=_params("parallel"),
        name="seq_dft_matrix",
    )(a_c, a_s, b_c, b_s)


def _cast_pad_cols_kernel(w_ref, o_ref):
    n = w_ref.shape[-1]
    o_ref[:, :n] = w_ref[...].astype(BF16)
    if o_ref.shape[-1] > n:
        o_ref[:, n:] = jnp.zeros((o_ref.shape[0], o_ref.shape[-1] - n), BF16)


def _cast_pad_rows_kernel(w_ref, o_ref):
    n = w_ref.shape[0]
    o_ref[:n, :] = w_ref[...].astype(BF16)
    if o_ref.shape[0] > n:
        o_ref[n:, :] = jnp.zeros((o_ref.shape[0] - n, o_ref.shape[1]), BF16)


def _prep_ffn_weights(w_in, w_out):
    depth, d, two_ff = w_in.shape
    d_ff = two_ff // 2
    fp = -(-d_ff // FF_ALIGN) * FF_ALIGN
    assert d_ff % LANES == 0
    tr = min(d, PREP_ROWS)
    w_ab = pl.pallas_call(
        _cast_pad_cols_kernel,
        grid=(depth, 2, d // tr),
        in_specs=[pl.BlockSpec((None, tr, d_ff), lambda l, h, i: (l, i, h))],
        out_specs=pl.BlockSpec((None, None, tr, fp), lambda l, h, i: (l, h, i, 0)),
        out_shape=jax.ShapeDtypeStruct((depth, 2, d, fp), BF16),
        compiler_params=_params("parallel", "parallel", "parallel"),
        name="prep_w_in",
    )(w_in)
    tc = min(d, PREP_ROWS)
    w_o = pl.pallas_call(
        _cast_pad_rows_kernel,
        grid=(depth, d // tc),
        in_specs=[pl.BlockSpec((None, d_ff, tc), lambda l, i: (l, 0, i))],
        out_specs=pl.BlockSpec((None, fp, tc), lambda l, i: (l, 0, i)),
        out_shape=jax.ShapeDtypeStruct((depth, fp, d), BF16),
        compiler_params=_params("parallel", "parallel"),
        name="prep_w_out",
    )(w_out)
    return w_ab, w_o


def _mixer_weights(w, d_hgrn):
    col = lax.broadcasted_iota(jnp.int32, (1, 1, w.shape[2]), 2)
    lo = D_FOURIER + 2 * d_hgrn
    scale = jnp.where((col >= lo) & (col < lo + 2 * d_hgrn), 0.5, 1.0)
    return (w * scale).astype(BF16)


def kernel(x, c, w_ada, b_ada, w_ffn1_in, w_ffn1_out, ln1_g, ln1_b, w_in, lower_bounds, fourier_g, hgrn_g,
           w_out, ln2_g, ln2_b, w_ffn2_in, w_ffn2_out, ln3_g, ln3_b):
    bsz, s, d = x.shape
    depth = w_ada.shape[0]
    d_hgrn = hgrn_g.shape[1]
    d_in = w_in.shape[2]
    n_heads = d_hgrn // LANES
    assert fourier_g.shape[1] == D_FOURIER and d_in == D_FOURIER + 5 * d_hgrn
    alpha = float((2 * depth) ** 0.25)

    tm = min(s, 1024)
    tm_out = min(s, 512)
    tn_proj = next(t for t in (2048, 1024, 512) if d_in % t == 0)
    assert (2 * d_hgrn) % (PROJ_SUB_SLABS * LANES) == 0 and D_FOURIER == PROJ_SUB_SLABS * LANES
    assert s % (GLA_GROUP * CHUNK) == 0
    nb = 2 if bsz % 2 == 0 else 1
    tk_dft = min(2 * s, 2048)

    ada = _ada_proj(c, w_ada, b_ada).reshape(depth, bsz, N_MOD, d)

    dft_mat = _seq_dft_matrix(s)
    cos_c, sin_c = _dft_tables(LANES, LANES)
    w_chan = jnp.concatenate([cos_c, -sin_c], axis=1).astype(BF16)
    dft_scale = float((s * LANES) ** -0.5)

    w_ab1, w_o1 = _prep_ffn_weights(w_ffn1_in, w_ffn1_out)
    w_ab2, w_o2 = _prep_ffn_weights(w_ffn2_in, w_ffn2_out)
    w_mix_in = _mixer_weights(w_in, d_hgrn)
    w_mix_out = w_out.astype(BF16)

    for l in range(depth):
        ada_l = ada[l]
        x = _ffn(x, ada_l, w_ab1, w_o1, ln1_g[l][None], ln1_b[l][None],
                 layer=l, mod0=0, alpha=alpha, tm=tm, tf=FF_ALIGN)

        proj, ab = _in_proj(x, ada_l, w_mix_in, w_chan,
                            layer=l, mod0=3, tm=tm, tn=tn_proj, n_heads=n_heads)
        yf = _seq_dft(dft_mat, ab.reshape(2 * s, bsz * D_FOURIER), fourier_g[l][None],
                      bsz=bsz, scale=dft_scale, tm=tm, nb=nb, tk=tk_dft)
        yh = _gla(proj, lower_bounds, hgrn_g[l][None], layer=l, n_heads=n_heads, group=GLA_GROUP)
        x = _out_proj(yf, yh, x, ada_l, w_mix_out, ln2_g[l][None], ln2_b[l][None],
                      layer=l, mod0=3, alpha=alpha, tm=tm_out)

        x = _ffn(x, ada_l, w_ab2, w_o2, ln3_g[l][None], ln3_b[l][None],
                 layer=l, mod0=6, alpha=alpha, tm=tm, tf=FF_ALIGN)
    return x
```

```python
import functools

import jax
import jax.numpy as jnp
from jax import lax
from jax.experimental import pallas as pl
from jax.experimental.pallas import tpu as pltpu

F32 = jnp.float32
BF16 = jnp.bfloat16

LN_EPS = 1e-5
RMS_EPS = 1e-6
LOG2_E = 1.4426950408889634
CHUNK = 64
LANES = 128
N_MOD = 9
FOURIER_GROUPS = 4
D_FOURIER = FOURIER_GROUPS * LANES
FF_ALIGN = 512
PREP_ROWS = 256
FFN_COL_CHUNK = 1024
FFN_ROW_CHUNK = 256
GLA_GROUP = 16
CUMSUM_ROWS = 256
GLA_OUT_GROUP = 32
PROJ_SUB_SLABS = 4
VMEM_LIMIT = 60 * 1024 * 1024


def _params(*sem):
    return pltpu.CompilerParams(dimension_semantics=sem, vmem_limit_bytes=VMEM_LIMIT)


def _silu(z):
    return z * (0.5 * jnp.tanh(0.5 * z) + 0.5)


def _layer_norm(xf, g, b):
    mu = jnp.mean(xf, axis=-1, keepdims=True)
    xc = xf - mu
    var = jnp.mean(xc * xc, axis=-1, keepdims=True)
    return xc * lax.rsqrt(var + LN_EPS) * g + b


def _dot(a, b):
    return jnp.dot(a, b, preferred_element_type=F32)


def _dot_nt(a, b):
    return lax.dot_general(a, b, (((1,), (1,)), ((), ())), preferred_element_type=F32)


def _ada_kernel(c_ref, w_ref, b_ref, o_ref):
    c = c_ref[...]
    o_ref[0] = _dot(_silu(c).astype(BF16), w_ref[0].astype(BF16)) + b_ref[0]


def _ada_proj(c, w_ada, b_ada):
    depth, d, n = w_ada.shape
    bsz = c.shape[0]
    tn = min(d, 1024)
    assert n % tn == 0
    return pl.pallas_call(
        _ada_kernel,
        grid=(depth, n // tn),
        in_specs=[
            pl.BlockSpec((bsz, d), lambda l, j: (0, 0)),
            pl.BlockSpec((1, d, tn), lambda l, j: (l, 0, j)),
            pl.BlockSpec((1, 1, tn), lambda l, j: (l, 0, j)),
        ],
        out_specs=pl.BlockSpec((1, bsz, tn), lambda l, j: (l, 0, j)),
        out_shape=jax.ShapeDtypeStruct((depth, bsz, n), F32),
        compiler_params=_params("parallel", "parallel"),
        name="ada_proj",
    )(c, w_ada, b_ada.reshape(depth, 1, n))


def _ffn_kernel(x_ref, ada_ref, wab_ref, wo_ref, g_ref, b_ref, o_ref, h_ref, hid_ref, *, mod0, alpha):
    j = pl.program_id(2)
    n = pl.num_programs(2) - 1
    tm, d = o_ref.shape

    def up(slot):
        h = h_ref[...]
        a = _dot(h, wab_ref[0])
        b = _dot(h, wab_ref[1])
        hid_ref[slot] = (_silu(a) * b).astype(BF16)

    @pl.when(j == 0)
    def _():
        shift = ada_ref[mod0:mod0 + 1, :]
        scale = ada_ref[mod0 + 1:mod0 + 2, :]
        x = x_ref[...]
        h_ref[...] = (x * (1.0 + scale) + shift).astype(BF16)
        o_ref[...] = alpha * x
        up(0)

    coef = 0.5 * (1.0 + ada_ref[mod0 + 2:mod0 + 3, :])

    @pl.when((j > 0) & (j < n))
    def _():
        prev = hid_ref[(j - 1) % 2]
        up(j % 2)
        cw = min(d, FFN_COL_CHUNK)
        for c in range(0, d, cw):
            o_ref[:, c:c + cw] += coef[:, c:c + cw] * _dot(prev, wo_ref[:, c:c + cw])

    @pl.when(j == n)
    def _():
        rc = min(tm, FFN_ROW_CHUNK)
        for r in range(0, tm, rc):
            y = o_ref[r:r + rc, :] + coef * _dot(hid_ref[(j - 1) % 2, r:r + rc, :], wo_ref[...])
            o_ref[r:r + rc, :] = _layer_norm(y, g_ref[...], b_ref[...])


def _ffn(x, ada_l, w_ab, w_out, ln_g, ln_b, *, layer, mod0, alpha, tm, tf):
    bsz, s, d = x.shape
    n = w_out.shape[1] // tf
    kern = functools.partial(_ffn_kernel, mod0=mod0, alpha=alpha)
    return pl.pallas_call(
        kern,
        grid=(bsz, s // tm, n + 1),
        in_specs=[
            pl.BlockSpec((None, tm, d), lambda b, i, j: (b, i, 0)),
            pl.BlockSpec((None, N_MOD, d), lambda b, i, j: (b, 0, 0)),
            pl.BlockSpec((None, 2, d, tf), lambda b, i, j: (layer, 0, 0, jnp.minimum(j, n - 1))),
            pl.BlockSpec((None, tf, d), lambda b, i, j: (layer, jnp.maximum(j - 1, 0), 0)),
            pl.BlockSpec((1, d), lambda b, i, j: (0, 0)),
            pl.BlockSpec((1, d), lambda b, i, j: (0, 0)),
        ],
        out_specs=pl.BlockSpec((None, tm, d), lambda b, i, j: (b, i, 0)),
        out_shape=jax.ShapeDtypeStruct((bsz, s, d), F32),
        scratch_shapes=[pltpu.VMEM((tm, d), BF16), pltpu.VMEM((2, tm, tf), BF16)],
        compiler_params=_params("parallel", "parallel", "arbitrary"),
        name="ffn",
    )(x, ada_l, w_ab, w_out, ln_g, ln_b)


def _in_proj_kernel(x_ref, ada_ref, w_ref, wc_ref, o_ref, ab_ref, h_ref, *, mod0, n_heads):
    j = pl.program_id(2)
    q_lo, q_hi, gate_lo = FOURIER_GROUPS, FOURIER_GROUPS + n_heads, FOURIER_GROUPS + 4 * n_heads

    @pl.when(j == 0)
    def _():
        shift = ada_ref[mod0:mod0 + 1, :]
        scale = ada_ref[mod0 + 1:mod0 + 2, :]
        h_ref[...] = (x_ref[...] * (1.0 + scale) + shift).astype(BF16)

    n_slabs = o_ref.shape[0]
    sub = min(n_slabs, PROJ_SUB_SLABS)
    for c0 in range(0, n_slabs, sub):
        res = _dot(h_ref[...], w_ref[:, c0 * LANES:(c0 + sub) * LANES])
        slab = j * n_slabs + c0
        is_silu = ((slab >= q_lo) & (slab < q_hi)) | (slab >= gate_lo)
        res = jnp.where(is_silu, _silu(res), res)
        for c in range(sub):
            o_ref[c0 + c] = res[:, c * LANES:(c + 1) * LANES].astype(BF16)

    @pl.when(j == 0)
    def _():
        for g in range(FOURIER_GROUPS):
            u = o_ref[g]
            cs = _dot(u, wc_ref[...])
            ab_ref[0, :, g * LANES:(g + 1) * LANES] = cs[:, :LANES].astype(BF16)
            ab_ref[1, :, g * LANES:(g + 1) * LANES] = cs[:, LANES:].astype(BF16)


def _in_proj(x, ada_l, w, w_chan, *, layer, mod0, tm, tn, n_heads):
    bsz, s, d = x.shape
    n = w.shape[2]
    assert n_heads % PROJ_SUB_SLABS == 0 and FOURIER_GROUPS == PROJ_SUB_SLABS
    kern = functools.partial(_in_proj_kernel, mod0=mod0, n_heads=n_heads)
    return pl.pallas_call(
        kern,
        grid=(bsz, s // tm, n // tn),
        in_specs=[
            pl.BlockSpec((None, tm, d), lambda b, i, j: (b, i, 0)),
            pl.BlockSpec((None, N_MOD, d), lambda b, i, j: (b, 0, 0)),
            pl.BlockSpec((None, d, tn), lambda b, i, j: (layer, 0, j)),
            pl.BlockSpec((LANES, 2 * LANES), lambda b, i, j: (0, 0)),
        ],
        out_specs=[
            pl.BlockSpec((None, tn // LANES, tm, LANES), lambda b, i, j: (b, j, i, 0)),
            pl.BlockSpec((2, tm, D_FOURIER), lambda b, i, j: (0, i, b)),
        ],
        out_shape=[
            jax.ShapeDtypeStruct((bsz, n // LANES, s, LANES), BF16),
            jax.ShapeDtypeStruct((2, s, bsz * D_FOURIER), BF16),
        ],
        scratch_shapes=[pltpu.VMEM((tm, d), BF16)],
        compiler_params=_params("parallel", "parallel", "arbitrary"),
        name="in_proj",
    )(x, ada_l, w, w_chan)


def _seq_dft_kernel(m_ref, ab_ref, g_ref, o_ref, acc_ref, *, scale, nb):
    k = pl.program_id(2)

    @pl.when(k == 0)
    def _():
        acc_ref[...] = jnp.zeros_like(acc_ref)

    acc_ref[...] += _dot(m_ref[...], ab_ref[...])

    @pl.when(k == pl.num_programs(2) - 1)
    def _():
        for bb in range(nb):
            for g in range(FOURIER_GROUPS):
                c0 = (bb * FOURIER_GROUPS + g) * LANES
                y = acc_ref[:, c0:c0 + LANES] * scale
                y = y * lax.rsqrt(jnp.mean(y * y, axis=-1, keepdims=True) + RMS_EPS)
                o_ref[bb, g] = (y * g_ref[:, g * LANES:(g + 1) * LANES]).astype(BF16)


def _seq_dft(dft_mat, ab, gain, *, bsz, scale, tm, nb, tk):
    s = dft_mat.shape[0]
    k_total = dft_mat.shape[1]
    tn = nb * D_FOURIER
    kern = functools.partial(_seq_dft_kernel, scale=scale, nb=nb)
    return pl.pallas_call(
        kern,
        grid=(bsz // nb, s // tm, k_total // tk),
        in_specs=[
            pl.BlockSpec((tm, tk), lambda n, i, k: (i, k)),
            pl.BlockSpec((tk, tn), lambda n, i, k: (k, n)),
            pl.BlockSpec((1, D_FOURIER), lambda n, i, k: (0, 0)),
        ],
        out_specs=pl.BlockSpec((nb, FOURIER_GROUPS, tm, LANES), lambda n, i, k: (n, 0, i, 0)),
        out_shape=jax.ShapeDtypeStruct((bsz, FOURIER_GROUPS, s, LANES), BF16),
        scratch_shapes=[pltpu.VMEM((tm, tn), F32)],
        compiler_params=_params("parallel", "parallel", "arbitrary"),
        name="seq_dft",
    )(dft_mat, ab, gain)


def _tri_cumsum(tri, x):
    x16 = x.astype(BF16)
    n = tri.shape[0]
    return jnp.concatenate([_dot(tri, x16[r:r + n, :]) for r in range(0, x.shape[0], n)], axis=0)


def _gla_kernel(q_ref, gt_ref, v_ref, zf_ref, zb_ref, lb_ref, gain_ref, tl_ref, tu_ref, o_ref,
                p_ref, kv_ref, qst_ref, sall_ref, d_ref, *, layer, group, out_group):
    s = q_ref.shape[0]
    n_chunks = s // CHUNK
    rows_g = group * CHUNK
    n_groups = n_chunks // group

    lb_raw = lb_ref[...]
    lb_exp = jnp.exp(lb_raw - jnp.max(lb_raw, axis=0, keepdims=True))
    lb_soft = lb_exp / jnp.sum(lb_exp, axis=0, keepdims=True)
    lbs = jnp.sum(lb_soft[:layer + 1], axis=0) - lb_soft[0]
    lb_f = lbs[0:1, :]
    lb_b = lbs[1:2, :]

    row = lax.broadcasted_iota(jnp.int32, (CHUNK, CHUNK), 0)
    col = lax.broadcasted_iota(jnp.int32, (CHUNK, CHUNK), 1)
    lower = row >= col
    upper = row <= col

    def gate_terms(zh_ref, rows, lb):
        half_span = 0.5 * (1.0 - lb)
        t = half_span * jnp.tanh(zh_ref[rows, :].astype(F32))
        return half_span - t, jnp.log((lb + half_span) + t)

    def scaled(q3, k3, b3, i_ref, i_last):
        b_ref = b3[:, i_ref:i_ref + 1, :]
        b_last = b3[:, i_last:i_last + 1, :]
        q_in = q3 * jnp.exp2(b3 - b_ref)
        k_in = k3 * jnp.exp2(b_ref - b3)
        q_st = q_in * jnp.exp2(b_ref)
        k_st = k_in * jnp.exp2(b_last - b_ref)
        return q_in, k_in, q_st, k_st, jnp.exp2(b_last)

    def pass0(gi, carry):
        r0 = pl.multiple_of(gi * rows_g, rows_g)
        rows = pl.ds(r0, rows_g)
        q3 = q_ref[rows, :].astype(F32).reshape(group, CHUNK, LANES)
        v = v_ref[rows, :]
        k_f, logf_f = gate_terms(zf_ref, rows, lb_f)
        k_b, logf_b = gate_terms(zb_ref, rows, lb_b)
        b_f = _tri_cumsum(tl_ref[...], logf_f) * LOG2_E
        b_b = _tri_cumsum(tu_ref[...], logf_b) * LOG2_E
        shape3 = (group, CHUNK, LANES)
        qf_in, kf_in, qf_st, kf_st, d_f = scaled(q3, k_f.reshape(shape3), b_f.reshape(shape3),
                                                 CHUNK // 2, CHUNK - 1)
        qb_in, kb_in, qb_st, kb_st, d_b = scaled(q3, k_b.reshape(shape3), b_b.reshape(shape3),
                                                 CHUNK - 1 - CHUNK // 2, 0)
        qst = jnp.concatenate([qf_st.reshape(rows_g, LANES), qb_st.reshape(rows_g, LANES)], axis=1)
        qst_ref[rows, :] = qst.astype(BF16)
        for g in range(group):
            c = gi * group + g
            s_f = _dot_nt(qf_in[g].astype(BF16), kf_in[g].astype(BF16))
            s_b = _dot_nt(qb_in[g].astype(BF16), kb_in[g].astype(BF16))
            p_ref[c] = (jnp.where(lower, s_f, 0.0) + jnp.where(upper, s_b, 0.0)).astype(BF16)
            k_st = jnp.concatenate([kf_st[g], kb_st[g]], axis=1).astype(BF16)
            v_t = v[g * CHUNK:(g + 1) * CHUNK, :].T
            kv_ref[c] = _dot(v_t, k_st)
            d_ref[c] = jnp.concatenate([d_f[g], d_b[g]], axis=1)
        return carry

    lax.fori_loop(0, n_groups, pass0, 0)

    def pass1(i, carry):
        st_f, st_b = carry
        cf = i
        cb = n_chunks - 1 - i
        sall_ref[cf, :, 0:LANES] = st_f.astype(BF16)
        sall_ref[cb, :, LANES:2 * LANES] = st_b.astype(BF16)
        st_f = d_ref[cf, :, 0:LANES] * st_f + kv_ref[cf, :, 0:LANES]
        st_b = d_ref[cb, :, LANES:2 * LANES] * st_b + kv_ref[cb, :, LANES:2 * LANES]
        return st_f, st_b

    zero = jnp.zeros((LANES, LANES), F32)
    lax.fori_loop(0, n_chunks, pass1, (zero, zero), unroll=2)

    gain = gain_ref[...]

    rows_o = out_group * CHUNK

    def pass2(gi, carry):
        r0 = pl.multiple_of(gi * rows_o, rows_o)
        rows = pl.ds(r0, rows_o)
        outs = []
        for g in range(out_group):
            c = gi * out_group + g
            rc = pl.ds(pl.multiple_of(r0 + g * CHUNK, CHUNK), CHUNK)
            o = _dot(p_ref[c], v_ref[rc, :]) + _dot_nt(qst_ref[rc, :], sall_ref[c])
            outs.append(o)
        o = jnp.concatenate(outs, axis=0)
        y = o * lax.rsqrt(jnp.mean(o * o, axis=-1, keepdims=True) + RMS_EPS) * gain
        o_ref[rows, :] = (y * gt_ref[rows, :].astype(F32)).astype(BF16)
        return carry

    lax.fori_loop(0, n_chunks // out_group, pass2, 0)


def _block_tri(n, upper):
    r = lax.broadcasted_iota(jnp.int32, (n, n), 0)
    c = lax.broadcasted_iota(jnp.int32, (n, n), 1)
    same = (r // CHUNK) == (c // CHUNK)
    tri = (r <= c) if upper else (r >= c)
    return (same & tri).astype(BF16)


def _gla(proj, lower_bounds, hgrn_g_l, *, layer, n_heads, group):
    bsz, _, s, _ = proj.shape
    depth = lower_bounds.shape[0]
    n_chunks = s // CHUNK
    rows_t = min(group * CHUNK, CUMSUM_ROWS)

    def slab(j):
        return pl.BlockSpec((None, None, s, LANES), lambda b, h: (b, FOURIER_GROUPS + j * n_heads + h, 0, 0))

    out_group = GLA_OUT_GROUP if n_chunks % GLA_OUT_GROUP == 0 else group
    kern = functools.partial(_gla_kernel, layer=layer, group=group, out_group=out_group)
    return pl.pallas_call(
        kern,
        grid=(bsz, n_heads),
        in_specs=[
            slab(0), slab(4), slab(1), slab(2), slab(3),
            pl.BlockSpec((depth, 2, LANES), lambda b, h: (0, 0, h)),
            pl.BlockSpec((1, LANES), lambda b, h: (0, h)),
            pl.BlockSpec((rows_t, rows_t), lambda b, h: (0, 0)),
            pl.BlockSpec((rows_t, rows_t), lambda b, h: (0, 0)),
        ],
        out_specs=pl.BlockSpec((None, None, s, LANES), lambda b, h: (b, h, 0, 0)),
        out_shape=jax.ShapeDtypeStruct((bsz, n_heads, s, LANES), BF16),
        scratch_shapes=[
            pltpu.VMEM((n_chunks, CHUNK, CHUNK), BF16),
            pltpu.VMEM((n_chunks, LANES, 2 * LANES), F32),
            pltpu.VMEM((s, 2 * LANES), BF16),
            pltpu.VMEM((n_chunks, LANES, 2 * LANES), BF16),
            pltpu.VMEM((n_chunks, 1, 2 * LANES), F32),
        ],
        compiler_params=_params("parallel", "parallel"),
        name="gla",
    )(proj, proj, proj, proj, proj, lower_bounds, hgrn_g_l,
      _block_tri(rows_t, False), _block_tri(rows_t, True))


def _out_proj_kernel(yf_ref, yh_ref, x_ref, ada_ref, w_ref, g_ref, b_ref, o_ref, *, mod0, alpha):
    coef = 1.0 + ada_ref[mod0 + 2:mod0 + 3, :]
    tm = o_ref.shape[0]
    rc = min(tm, FFN_ROW_CHUNK)
    for r in range(0, tm, rc):
        parts = ([yf_ref[g, r:r + rc, :] for g in range(yf_ref.shape[0])]
                 + [yh_ref[h, r:r + rc, :] for h in range(yh_ref.shape[0])])
        y = _dot(jnp.concatenate(parts, axis=-1), w_ref[...])
        o_ref[r:r + rc, :] = _layer_norm(alpha * x_ref[r:r + rc, :] + coef * y, g_ref[...], b_ref[...])


def _out_proj(yf, yh, x, ada_l, w, ln_g, ln_b, *, layer, mod0, alpha, tm):
    bsz, s, d = x.shape
    n_heads = yh.shape[1]
    kern = functools.partial(_out_proj_kernel, mod0=mod0, alpha=alpha)
    return pl.pallas_call(
        kern,
        grid=(bsz, s // tm),
        in_specs=[
            pl.BlockSpec((None, FOURIER_GROUPS, tm, LANES), lambda b, i: (b, 0, i, 0)),
            pl.BlockSpec((None, n_heads, tm, LANES), lambda b, i: (b, 0, i, 0)),
            pl.BlockSpec((None, tm, d), lambda b, i: (b, i, 0)),
            pl.BlockSpec((None, N_MOD, d), lambda b, i: (b, 0, 0)),
            pl.BlockSpec((None,) + w.shape[1:], lambda b, i: (layer, 0, 0)),
            pl.BlockSpec((1, d), lambda b, i: (0, 0)),
            pl.BlockSpec((1, d), lambda b, i: (0, 0)),
        ],
        out_specs=pl.BlockSpec((None, tm, d), lambda b, i: (b, i, 0)),
        out_shape=jax.ShapeDtypeStruct((bsz, s, d), F32),
        compiler_params=_params("parallel", "parallel"),
        name="out_proj",
    )(yf, yh, x, ada_l, w, ln_g, ln_b)


def _dft_tables(s, n_cols, stride=1):
    k = lax.broadcasted_iota(jnp.int32, (s, n_cols), 0)
    n = lax.broadcasted_iota(jnp.int32, (s, n_cols), 1)
    ang = ((k * (n * stride)) % s).astype(F32) * (2.0 * jnp.pi / s)
    return jnp.cos(ang), jnp.sin(ang)


def _seq_dft_matrix_kernel(ac_ref, as_ref, bc_ref, bs_ref, o_ref):
    s = o_ref.shape[1] // 2
    b_c = bc_ref[...]
    b_s = bs_ref[...]
    for n1 in range(s // LANES):
        a_c = ac_ref[:, n1:n1 + 1]
        a_s = as_ref[:, n1:n1 + 1]
        o_ref[:, n1 * LANES:(n1 + 1) * LANES] = (a_c * b_c - a_s * b_s).astype(BF16)
        o_ref[:, s + n1 * LANES:s + (n1 + 1) * LANES] = (a_s * b_c + a_c * b_s).astype(BF16)


def _seq_dft_matrix(s):
    n_coarse = s // LANES
    a_c, a_s = _dft_tables(s, n_coarse, stride=LANES)
    b_c, b_s = _dft_tables(s, LANES)
    tm = min(s, PREP_ROWS)
    coarse = pl.BlockSpec((tm, n_coarse), lambda i: (i, 0))
    fine = pl.BlockSpec((tm, LANES), lambda i: (i, 0))
    return pl.pallas_call(
        _seq_dft_matrix_kernel,
        grid=(s // tm,),
        in_specs=[coarse, coarse, fine, fine],
        out_specs=pl.BlockSpec((tm, 2 * s), lambda i: (i, 0)),
        out_shape=jax.ShapeDtypeStruct((s, 2 * s), BF16),
        compiler_params=_params("parallel"),
        name="seq_dft_matrix",
    )(a_c, a_s, b_c, b_s)


def _cast_pad_cols_kernel(w_ref, o_ref):
    n = w_ref.shape[-1]
    o_ref[:, :n] = w_ref[...].astype(BF16)
    if o_ref.shape[-1] > n:
        o_ref[:, n:] = jnp.zeros((o_ref.shape[0], o_ref.shape[-1] - n), BF16)


def _cast_pad_rows_kernel(w_ref, o_ref):
    n = w_ref.shape[0]
    o_ref[:n, :] = w_ref[...].astype(BF16)
    if o_ref.shape[0] > n:
        o_ref[n:, :] = jnp.zeros((o_ref.shape[0] - n, o_ref.shape[1]), BF16)


def _prep_ffn_weights(w_in, w_out):
    depth, d, two_ff = w_in.shape
    d_ff = two_ff // 2
    fp = -(-d_ff // FF_ALIGN) * FF_ALIGN
    assert d_ff % LANES == 0
    tr = min(d, PREP_ROWS)
    w_ab = pl.pallas_call(
        _cast_pad_cols_kernel,
        grid=(depth, 2, d // tr),
        in_specs=[pl.BlockSpec((None, tr, d_ff), lambda l, h, i: (l, i, h))],
        out_specs=pl.BlockSpec((None, None, tr, fp), lambda l, h, i: (l, h, i, 0)),
        out_shape=jax.ShapeDtypeStruct((depth, 2, d, fp), BF16),
        compiler_params=_params("parallel", "parallel", "parallel"),
        name="prep_w_in",
    )(w_in)
    tc = min(d, PREP_ROWS)
    w_o = pl.pallas_call(
        _cast_pad_rows_kernel,
        grid=(depth, d // tc),
        in_specs=[pl.BlockSpec((None, d_ff, tc), lambda l, i: (l, 0, i))],
        out_specs=pl.BlockSpec((None, fp, tc), lambda l, i: (l, 0, i)),
        out_shape=jax.ShapeDtypeStruct((depth, fp, d), BF16),
        compiler_params=_params("parallel", "parallel"),
        name="prep_w_out",
    )(w_out)
    return w_ab, w_o


def _mixer_weights(w, d_hgrn):
    col = lax.broadcasted_iota(jnp.int32, (1, 1, w.shape[2]), 2)
    lo = D_FOURIER + 2 * d_hgrn
    scale = jnp.where((col >= lo) & (col < lo + 2 * d_hgrn), 0.5, 1.0)
    return (w * scale).astype(BF16)


def kernel(x, c, w_ada, b_ada, w_ffn1_in, w_ffn1_out, ln1_g, ln1_b, w_in, lower_bounds, fourier_g, hgrn_g,
           w_out, ln2_g, ln2_b, w_ffn2_in, w_ffn2_out, ln3_g, ln3_b):
    bsz, s, d = x.shape
    depth = w_ada.shape[0]
    d_hgrn = hgrn_g.shape[1]
    d_in = w_in.shape[2]
    n_heads = d_hgrn // LANES
    assert fourier_g.shape[1] == D_FOURIER and d_in == D_FOURIER + 5 * d_hgrn
    alpha = float((2 * depth) ** 0.25)

    tm = min(s, 1024)
    tm_out = min(s, 512)
    tn_proj = next(t for t in (2048, 1024, 512) if d_in % t == 0)
    assert (2 * d_hgrn) % (PROJ_SUB_SLABS * LANES) == 0 and D_FOURIER == PROJ_SUB_SLABS * LANES
    assert s % (GLA_GROUP * CHUNK) == 0
    nb = 2 if bsz % 2 == 0 else 1
    tk_dft = min(2 * s, 2048)

    ada = _ada_proj(c, w_ada, b_ada).reshape(depth, bsz, N_MOD, d)

    dft_mat = _seq_dft_matrix(s)
    cos_c, sin_c = _dft_tables(LANES, LANES)
    w_chan = jnp.concatenate([cos_c, -sin_c], axis=1).astype(BF16)
    dft_scale = float((s * LANES) ** -0.5)

    w_ab1, w_o1 = _prep_ffn_weights(w_ffn1_in, w_ffn1_out)
    w_ab2, w_o2 = _prep_ffn_weights(w_ffn2_in, w_ffn2_out)
    w_mix_in = _mixer_weights(w_in, d_hgrn)
    w_mix_out = w_out.astype(BF16)

    for l in range(depth):
        ada_l = ada[l]
        x = _ffn(x, ada_l, w_ab1, w_o1, ln1_g[l][None], ln1_b[l][None],
                 layer=l, mod0=0, alpha=alpha, tm=tm, tf=FF_ALIGN)

        proj, ab = _in_proj(x, ada_l, w_mix_in, w_chan,
                            layer=l, mod0=3, tm=tm, tn=tn_proj, n_heads=n_heads)
        yf = _seq_dft(dft_mat, ab.reshape(2 * s, bsz * D_FOURIER), fourier_g[l][None],
                      bsz=bsz, scale=dft_scale, tm=tm, nb=nb, tk=tk_dft)
        yh = _gla(proj, lower_bounds, hgrn_g[l][None], layer=l, n_heads=n_heads, group=GLA_GROUP)
        x = _out_proj(yf, yh, x, ada_l, w_mix_out, ln2_g[l][None], ln2_b[l][None],
                      layer=l, mod0=3, alpha=alpha, tm=tm_out)

        x = _ffn(x, ada_l, w_ab2, w_o2, ln3_g[l][None], ln3_b[l][None],
                 layer=l, mod0=6, alpha=alpha, tm=tm, tf=FF_ALIGN)
    return x
```

```python
import functools

import jax
import jax.numpy as jnp
from jax import lax
from jax.experimental import pallas as pl
from jax.experimental.pallas import tpu as pltpu

F32 = jnp.float32
BF16 = jnp.bfloat16

LN_EPS = 1e-5
RMS_EPS = 1e-6
LOG2_E = 1.4426950408889634
CHUNK = 64
LANES = 128
N_MOD = 9
FOURIER_GROUPS = 4
D_FOURIER = FOURIER_GROUPS * LANES
FF_ALIGN = 512
PREP_ROWS = 256
FFN_COL_CHUNK = 1024
FFN_ROW_CHUNK = 256
GLA_GROUP = 16
CUMSUM_ROWS = 256
GLA_OUT_GROUP = 32
PROJ_SUB_SLABS = 4
VMEM_LIMIT = 60 * 1024 * 1024


def _params(*sem):
    return pltpu.CompilerParams(dimension_semantics=sem, vmem_limit_bytes=VMEM_LIMIT)


def _silu(z):
    return z * (0.5 * jnp.tanh(0.5 * z) + 0.5)


def _layer_norm(xf, g, b):
    mu = jnp.mean(xf, axis=-1, keepdims=True)
    xc = xf - mu
    var = jnp.mean(xc * xc, axis=-1, keepdims=True)
    return xc * lax.rsqrt(var + LN_EPS) * g + b


def _x_lookahead_map(n_batch, n_tiles):
    def index_map(b, i, j):
        t = jnp.minimum(b * n_tiles + i + (j > 0).astype(jnp.int32), n_batch * n_tiles - 1)
        return (t // n_tiles, t % n_tiles, 0)
    return index_map


def _dot(a, b):
    return jnp.dot(a, b, preferred_element_type=F32)


def _dot_nt(a, b):
    return lax.dot_general(a, b, (((1,), (1,)), ((), ())), preferred_element_type=F32)


def _ada_kernel(c_ref, w_ref, b_ref, o_ref):
    c = c_ref[...]
    o_ref[0] = _dot(_silu(c).astype(BF16), w_ref[0].astype(BF16)) + b_ref[0]


def _ada_proj(c, w_ada, b_ada):
    depth, d, n = w_ada.shape
    bsz = c.shape[0]
    tn = min(d, 1024)
    assert n % tn == 0
    return pl.pallas_call(
        _ada_kernel,
        grid=(depth, n // tn),
        in_specs=[
            pl.BlockSpec((bsz, d), lambda l, j: (0, 0)),
            pl.BlockSpec((1, d, tn), lambda l, j: (l, 0, j)),
            pl.BlockSpec((1, 1, tn), lambda l, j: (l, 0, j)),
        ],
        out_specs=pl.BlockSpec((1, bsz, tn), lambda l, j: (l, 0, j)),
        out_shape=jax.ShapeDtypeStruct((depth, bsz, n), F32),
        compiler_params=_params("parallel", "parallel"),
        name="ada_proj",
    )(c, w_ada, b_ada.reshape(depth, 1, n))


def _ffn_kernel(x_ref, ada_ref, wab_ref, wo_ref, g_ref, b_ref, o_ref, h_ref, hid_ref, *, mod0, alpha):
    j = pl.program_id(2)
    n = pl.num_programs(2) - 1
    tm, d = o_ref.shape

    def up(slot):
        h = h_ref[...]
        a = _dot(h, wab_ref[0])
        b = _dot(h, wab_ref[1])
        hid_ref[slot] = (_silu(a) * b).astype(BF16)

    @pl.when(j == 0)
    def _():
        shift = ada_ref[mod0:mod0 + 1, :]
        scale = ada_ref[mod0 + 1:mod0 + 2, :]
        x = x_ref[...]
        h_ref[...] = (x * (1.0 + scale) + shift).astype(BF16)
        o_ref[...] = alpha * x
        up(0)

    coef = 0.5 * (1.0 + ada_ref[mod0 + 2:mod0 + 3, :])

    @pl.when((j > 0) & (j < n))
    def _():
        prev = hid_ref[(j - 1) % 2]
        up(j % 2)
        cw = min(d, FFN_COL_CHUNK)
        for c in range(0, d, cw):
            o_ref[:, c:c + cw] += coef[:, c:c + cw] * _dot(prev, wo_ref[:, c:c + cw])

    @pl.when(j == n)
    def _():
        rc = min(tm, FFN_ROW_CHUNK)
        for r in range(0, tm, rc):
            y = o_ref[r:r + rc, :] + coef * _dot(hid_ref[(j - 1) % 2, r:r + rc, :], wo_ref[...])
            o_ref[r:r + rc, :] = _layer_norm(y, g_ref[...], b_ref[...])


def _ffn(x, ada_l, w_ab, w_out, ln_g, ln_b, *, layer, mod0, alpha, tm, tf):
    bsz, s, d = x.shape
    n = w_out.shape[1] // tf
    kern = functools.partial(_ffn_kernel, mod0=mod0, alpha=alpha)
    return pl.pallas_call(
        kern,
        grid=(bsz, s // tm, n + 1),
        in_specs=[
            pl.BlockSpec((None, tm, d), _x_lookahead_map(bsz, s // tm)),
            pl.BlockSpec((None, N_MOD, d), lambda b, i, j: (b, 0, 0)),
            pl.BlockSpec((None, 2, d, tf), lambda b, i, j: (layer, 0, 0, jnp.minimum(j, n - 1))),
            pl.BlockSpec((None, tf, d), lambda b, i, j: (layer, jnp.maximum(j - 1, 0), 0)),
            pl.BlockSpec((1, d), lambda b, i, j: (0, 0)),
            pl.BlockSpec((1, d), lambda b, i, j: (0, 0)),
        ],
        out_specs=pl.BlockSpec((None, tm, d), lambda b, i, j: (b, i, 0)),
        out_shape=jax.ShapeDtypeStruct((bsz, s, d), F32),
        scratch_shapes=[pltpu.VMEM((tm, d), BF16), pltpu.VMEM((2, tm, tf), BF16)],
        compiler_params=_params("parallel", "parallel", "arbitrary"),
        name="ffn",
    )(x, ada_l, w_ab, w_out, ln_g, ln_b)


def _in_proj_kernel(x_ref, ada_ref, w_ref, wc_ref, o_ref, ab_ref, h_ref, *, mod0, n_heads):
    j = pl.program_id(2)
    q_lo, q_hi, gate_lo = FOURIER_GROUPS, FOURIER_GROUPS + n_heads, FOURIER_GROUPS + 4 * n_heads

    @pl.when(j == 0)
    def _():
        shift = ada_ref[mod0:mod0 + 1, :]
        scale = ada_ref[mod0 + 1:mod0 + 2, :]
        h_ref[...] = (x_ref[...] * (1.0 + scale) + shift).astype(BF16)

    n_slabs = o_ref.shape[0]
    sub = min(n_slabs, PROJ_SUB_SLABS)
    for c0 in range(0, n_slabs, sub):
        res = _dot(h_ref[...], w_ref[:, c0 * LANES:(c0 + sub) * LANES])
        slab = j * n_slabs + c0
        is_silu = ((slab >= q_lo) & (slab < q_hi)) | (slab >= gate_lo)
        res = jnp.where(is_silu, _silu(res), res)
        for c in range(sub):
            o_ref[c0 + c] = res[:, c * LANES:(c + 1) * LANES].astype(BF16)

    @pl.when(j == 0)
    def _():
        for g in range(FOURIER_GROUPS):
            u = o_ref[g]
            cs = _dot(u, wc_ref[...])
            ab_ref[0, :, g * LANES:(g + 1) * LANES] = cs[:, :LANES].astype(BF16)
            ab_ref[1, :, g * LANES:(g + 1) * LANES] = cs[:, LANES:].astype(BF16)


def _in_proj(x, ada_l, w, w_chan, *, layer, mod0, tm, tn, n_heads):
    bsz, s, d = x.shape
    n = w.shape[2]
    assert n_heads % PROJ_SUB_SLABS == 0 and FOURIER_GROUPS == PROJ_SUB_SLABS
    kern = functools.partial(_in_proj_kernel, mod0=mod0, n_heads=n_heads)
    return pl.pallas_call(
        kern,
        grid=(bsz, s // tm, n // tn),
        in_specs=[
            pl.BlockSpec((None, tm, d), _x_lookahead_map(bsz, s // tm)),
            pl.BlockSpec((None, N_MOD, d), lambda b, i, j: (b, 0, 0)),
            pl.BlockSpec((None, d, tn), lambda b, i, j: (layer, 0, j)),
            pl.BlockSpec((LANES, 2 * LANES), lambda b, i, j: (0, 0)),
        ],
        out_specs=[
            pl.BlockSpec((None, tn // LANES, tm, LANES), lambda b, i, j: (b, j, i, 0)),
            pl.BlockSpec((2, tm, D_FOURIER), lambda b, i, j: (0, i, b)),
        ],
        out_shape=[
            jax.ShapeDtypeStruct((bsz, n // LANES, s, LANES), BF16),
            jax.ShapeDtypeStruct((2, s, bsz * D_FOURIER), BF16),
        ],
        scratch_shapes=[pltpu.VMEM((tm, d), BF16)],
        compiler_params=_params("parallel", "parallel", "arbitrary"),
        name="in_proj",
    )(x, ada_l, w, w_chan)


def _seq_dft_kernel(m_ref, ab_ref, g_ref, o_ref, acc_ref, *, scale, nb):
    k = pl.program_id(2)

    @pl.when(k == 0)
    def _():
        acc_ref[...] = jnp.zeros_like(acc_ref)

    acc_ref[...] += _dot(m_ref[...], ab_ref[...])

    @pl.when(k == pl.num_programs(2) - 1)
    def _():
        for bb in range(nb):
            for g in range(FOURIER_GROUPS):
                c0 = (bb * FOURIER_GROUPS + g) * LANES
                y = acc_ref[:, c0:c0 + LANES] * scale
                y = y * lax.rsqrt(jnp.mean(y * y, axis=-1, keepdims=True) + RMS_EPS)
                o_ref[bb, g] = (y * g_ref[:, g * LANES:(g + 1) * LANES]).astype(BF16)


def _seq_dft(dft_mat, ab, gain, *, bsz, scale, tm, nb, tk):
    s = dft_mat.shape[0]
    k_total = dft_mat.shape[1]
    tn = nb * D_FOURIER
    kern = functools.partial(_seq_dft_kernel, scale=scale, nb=nb)
    return pl.pallas_call(
        kern,
        grid=(bsz // nb, s // tm, k_total // tk),
        in_specs=[
            pl.BlockSpec((tm, tk), lambda n, i, k: (i, k)),
            pl.BlockSpec((tk, tn), lambda n, i, k: (k, n)),
            pl.BlockSpec((1, D_FOURIER), lambda n, i, k: (0, 0)),
        ],
        out_specs=pl.BlockSpec((nb, FOURIER_GROUPS, tm, LANES), lambda n, i, k: (n, 0, i, 0)),
        out_shape=jax.ShapeDtypeStruct((bsz, FOURIER_GROUPS, s, LANES), BF16),
        scratch_shapes=[pltpu.VMEM((tm, tn), F32)],
        compiler_params=_params("parallel", "parallel", "arbitrary"),
        name="seq_dft",
    )(dft_mat, ab, gain)


def _tri_cumsum(tri, x):
    x16 = x.astype(BF16)
    n = tri.shape[0]
    return jnp.concatenate([_dot(tri, x16[r:r + n, :]) for r in range(0, x.shape[0], n)], axis=0)


def _gla_kernel(q_ref, gt_ref, v_ref, zf_ref, zb_ref, lb_ref, gain_ref, tl_ref, tu_ref, o_ref,
                p_ref, kv_ref, qst_ref, sall_ref, d_ref, *, layer, group, out_group):
    s = q_ref.shape[0]
    n_chunks = s // CHUNK
    rows_g = group * CHUNK
    n_groups = n_chunks // group

    lb_raw = lb_ref[...]
    lb_exp = jnp.exp(lb_raw - jnp.max(lb_raw, axis=0, keepdims=True))
    lb_soft = lb_exp / jnp.sum(lb_exp, axis=0, keepdims=True)
    lbs = jnp.sum(lb_soft[:layer + 1], axis=0) - lb_soft[0]
    lb_f = lbs[0:1, :]
    lb_b = lbs[1:2, :]

    row = lax.broadcasted_iota(jnp.int32, (CHUNK, CHUNK), 0)
    col = lax.broadcasted_iota(jnp.int32, (CHUNK, CHUNK), 1)
    lower = row >= col
    upper = row <= col

    def gate_terms(zh_ref, rows, lb):
        half_span = 0.5 * (1.0 - lb)
        t = half_span * jnp.tanh(zh_ref[rows, :].astype(F32))
        return half_span - t, jnp.log((lb + half_span) + t)

    def scaled(q3, k3, b3, i_ref, i_last):
        b_ref = b3[:, i_ref:i_ref + 1, :]
        b_last = b3[:, i_last:i_last + 1, :]
        q_in = q3 * jnp.exp2(b3 - b_ref)
        k_in = k3 * jnp.exp2(b_ref - b3)
        q_st = q_in * jnp.exp2(b_ref)
        k_st = k_in * jnp.exp2(b_last - b_ref)
        return q_in, k_in, q_st, k_st, jnp.exp2(b_last)

    def pass0(gi, carry):
        r0 = pl.multiple_of(gi * rows_g, rows_g)
        rows = pl.ds(r0, rows_g)
        q3 = q_ref[rows, :].astype(F32).reshape(group, CHUNK, LANES)
        v = v_ref[rows, :]
        k_f, logf_f = gate_terms(zf_ref, rows, lb_f)
        k_b, logf_b = gate_terms(zb_ref, rows, lb_b)
        b_f = _tri_cumsum(tl_ref[...], logf_f) * LOG2_E
        b_b = _tri_cumsum(tu_ref[...], logf_b) * LOG2_E
        shape3 = (group, CHUNK, LANES)
        qf_in, kf_in, qf_st, kf_st, d_f = scaled(q3, k_f.reshape(shape3), b_f.reshape(shape3),
                                                 CHUNK // 2, CHUNK - 1)
        qb_in, kb_in, qb_st, kb_st, d_b = scaled(q3, k_b.reshape(shape3), b_b.reshape(shape3),
                                                 CHUNK - 1 - CHUNK // 2, 0)
        qst = jnp.concatenate([qf_st.reshape(rows_g, LANES), qb_st.reshape(rows_g, LANES)], axis=1)
        qst_ref[rows, :] = qst.astype(BF16)
        for g in range(group):
            c = gi * group + g
            s_f = _dot_nt(qf_in[g].astype(BF16), kf_in[g].astype(BF16))
            s_b = _dot_nt(qb_in[g].astype(BF16), kb_in[g].astype(BF16))
            p_ref[c] = (jnp.where(lower, s_f, 0.0) + jnp.where(upper, s_b, 0.0)).astype(BF16)
            k_st = jnp.concatenate([kf_st[g], kb_st[g]], axis=1).astype(BF16)
            v_t = v[g * CHUNK:(g + 1) * CHUNK, :].T
            kv_ref[c] = _dot(v_t, k_st)
            d_ref[c] = jnp.concatenate([d_f[g], d_b[g]], axis=1)
        return carry

    lax.fori_loop(0, n_groups, pass0, 0)

    def pass1(i, carry):
        st_f, st_b = carry
        cf = i
        cb = n_chunks - 1 - i
        sall_ref[cf, :, 0:LANES] = st_f.astype(BF16)
        sall_ref[cb, :, LANES:2 * LANES] = st_b.astype(BF16)
        st_f = d_ref[cf, :, 0:LANES] * st_f + kv_ref[cf, :, 0:LANES]
        st_b = d_ref[cb, :, LANES:2 * LANES] * st_b + kv_ref[cb, :, LANES:2 * LANES]
        return st_f, st_b

    zero = jnp.zeros((LANES, LANES), F32)
    lax.fori_loop(0, n_chunks, pass1, (zero, zero), unroll=2)

    gain = gain_ref[...]

    rows_o = out_group * CHUNK

    def pass2(gi, carry):
        r0 = pl.multiple_of(gi * rows_o, rows_o)
        rows = pl.ds(r0, rows_o)
        outs = []
        for g in range(out_group):
            c = gi * out_group + g
            rc = pl.ds(pl.multiple_of(r0 + g * CHUNK, CHUNK), CHUNK)
            o = _dot(p_ref[c], v_ref[rc, :]) + _dot_nt(qst_ref[rc, :], sall_ref[c])
            outs.append(o)
        o = jnp.concatenate(outs, axis=0)
        y = o * lax.rsqrt(jnp.mean(o * o, axis=-1, keepdims=True) + RMS_EPS) * gain
        o_ref[rows, :] = (y * gt_ref[rows, :].astype(F32)).astype(BF16)
        return carry

    lax.fori_loop(0, n_chunks // out_group, pass2, 0)


def _block_tri(n, upper):
    r = lax.broadcasted_iota(jnp.int32, (n, n), 0)
    c = lax.broadcasted_iota(jnp.int32, (n, n), 1)
    same = (r // CHUNK) == (c // CHUNK)
    tri = (r <= c) if upper else (r >= c)
    return (same & tri).astype(BF16)


def _gla(proj, lower_bounds, hgrn_g_l, *, layer, n_heads, group):
    bsz, _, s, _ = proj.shape
    depth = lower_bounds.shape[0]
    n_chunks = s // CHUNK
    rows_t = min(group * CHUNK, CUMSUM_ROWS)

    def slab(j):
        return pl.BlockSpec((None, None, s, LANES), lambda b, h: (b, FOURIER_GROUPS + j * n_heads + h, 0, 0))

    out_group = GLA_OUT_GROUP if n_chunks % GLA_OUT_GROUP == 0 else group
    kern = functools.partial(_gla_kernel, layer=layer, group=group, out_group=out_group)
    return pl.pallas_call(
        kern,
        grid=(bsz, n_heads),
        in_specs=[
            slab(0), slab(4), slab(1), slab(2), slab(3),
            pl.BlockSpec((depth, 2, LANES), lambda b, h: (0, 0, h)),
            pl.BlockSpec((1, LANES), lambda b, h: (0, h)),
            pl.BlockSpec((rows_t, rows_t), lambda b, h: (0, 0)),
            pl.BlockSpec((rows_t, rows_t), lambda b, h: (0, 0)),
        ],
        out_specs=pl.BlockSpec((None, None, s, LANES), lambda b, h: (b, h, 0, 0)),
        out_shape=jax.ShapeDtypeStruct((bsz, n_heads, s, LANES), BF16),
        scratch_shapes=[
            pltpu.VMEM((n_chunks, CHUNK, CHUNK), BF16),
            pltpu.VMEM((n_chunks, LANES, 2 * LANES), F32),
            pltpu.VMEM((s, 2 * LANES), BF16),
            pltpu.VMEM((n_chunks, LANES, 2 * LANES), BF16),
            pltpu.VMEM((n_chunks, 1, 2 * LANES), F32),
        ],
        compiler_params=_params("parallel", "parallel"),
        name="gla",
    )(proj, proj, proj, proj, proj, lower_bounds, hgrn_g_l,
      _block_tri(rows_t, False), _block_tri(rows_t, True))


def _out_proj_kernel(yf_ref, yh_ref, x_ref, ada_ref, w_ref, g_ref, b_ref, o_ref, *, mod0, alpha):
    coef = 1.0 + ada_ref[mod0 + 2:mod0 + 3, :]
    tm = o_ref.shape[0]
    rc = min(tm, FFN_ROW_CHUNK)
    for r in range(0, tm, rc):
        parts = ([yf_ref[g, r:r + rc, :] for g in range(yf_ref.shape[0])]
                 + [yh_ref[h, r:r + rc, :] for h in range(yh_ref.shape[0])])
        y = _dot(jnp.concatenate(parts, axis=-1), w_ref[...])
        o_ref[r:r + rc, :] = _layer_norm(alpha * x_ref[r:r + rc, :] + coef * y, g_ref[...], b_ref[...])


def _out_proj(yf, yh, x, ada_l, w, ln_g, ln_b, *, layer, mod0, alpha, tm):
    bsz, s, d = x.shape
    n_heads = yh.shape[1]
    kern = functools.partial(_out_proj_kernel, mod0=mod0, alpha=alpha)
    return pl.pallas_call(
        kern,
        grid=(bsz, s // tm),
        in_specs=[
            pl.BlockSpec((None, FOURIER_GROUPS, tm, LANES), lambda b, i: (b, 0, i, 0)),
            pl.BlockSpec((None, n_heads, tm, LANES), lambda b, i: (b, 0, i, 0)),
            pl.BlockSpec((None, tm, d), lambda b, i: (b, i, 0)),
            pl.BlockSpec((None, N_MOD, d), lambda b, i: (b, 0, 0)),
            pl.BlockSpec((None,) + w.shape[1:], lambda b, i: (layer, 0, 0)),
            pl.BlockSpec((1, d), lambda b, i: (0, 0)),
            pl.BlockSpec((1, d), lambda b, i: (0, 0)),
        ],
        out_specs=pl.BlockSpec((None, tm, d), lambda b, i: (b, i, 0)),
        out_shape=jax.ShapeDtypeStruct((bsz, s, d), F32),
        compiler_params=_params("parallel", "parallel"),
        name="out_proj",
    )(yf, yh, x, ada_l, w, ln_g, ln_b)


def _dft_tables(s, n_cols, stride=1):
    k = lax.broadcasted_iota(jnp.int32, (s, n_cols), 0)
    n = lax.broadcasted_iota(jnp.int32, (s, n_cols), 1)
    ang = ((k * (n * stride)) % s).astype(F32) * (2.0 * jnp.pi / s)
    return jnp.cos(ang), jnp.sin(ang)


def _seq_dft_matrix_kernel(ac_ref, as_ref, bc_ref, bs_ref, o_ref):
    s = o_ref.shape[1] // 2
    b_c = bc_ref[...]
    b_s = bs_ref[...]
    for n1 in range(s // LANES):
        a_c = ac_ref[:, n1:n1 + 1]
        a_s = as_ref[:, n1:n1 + 1]
        o_ref[:, n1 * LANES:(n1 + 1) * LANES] = (a_c * b_c - a_s * b_s).astype(BF16)
        o_ref[:, s + n1 * LANES:s + (n1 + 1) * LANES] = (a_s * b_c + a_c * b_s).astype(BF16)


def _seq_dft_matrix(s):
    n_coarse = s // LANES
    a_c, a_s = _dft_tables(s, n_coarse, stride=LANES)
    b_c, b_s = _dft_tables(s, LANES)
    tm = min(s, PREP_ROWS)
    coarse = pl.BlockSpec((tm, n_coarse), lambda i: (i, 0))
    fine = pl.BlockSpec((tm, LANES), lambda i: (i, 0))
    return pl.pallas_call(
        _seq_dft_matrix_kernel,
        grid=(s // tm,),
        in_specs=[coarse, coarse, fine, fine],
        out_specs=pl.BlockSpec((tm, 2 * s), lambda i: (i, 0)),
        out_shape=jax.ShapeDtypeStruct((s, 2 * s), BF16),
        compiler_params=_params("parallel"),
        name="seq_dft_matrix",
    )(a_c, a_s, b_c, b_s)


def _cast_pad_cols_kernel(w_ref, o_ref):
    n = w_ref.shape[-1]
    o_ref[:, :n] = w_ref[...].astype(BF16)
    if o_ref.shape[-1] > n:
        o_ref[:, n:] = jnp.zeros((o_ref.shape[0], o_ref.shape[-1] - n), BF16)


def _cast_pad_rows_kernel(w_ref, o_ref):
    n = w_ref.shape[0]
    o_ref[:n, :] = w_ref[...].astype(BF16)
    if o_ref.shape[0] > n:
        o_ref[n:, :] = jnp.zeros((o_ref.shape[0] - n, o_ref.shape[1]), BF16)


def _prep_ffn_weights(w_in, w_out):
    depth, d, two_ff = w_in.shape
    d_ff = two_ff // 2
    fp = -(-d_ff // FF_ALIGN) * FF_ALIGN
    assert d_ff % LANES == 0
    tr = min(d, PREP_ROWS)
    w_ab = pl.pallas_call(
        _cast_pad_cols_kernel,
        grid=(depth, 2, d // tr),
        in_specs=[pl.BlockSpec((None, tr, d_ff), lambda l, h, i: (l, i, h))],
        out_specs=pl.BlockSpec((None, None, tr, fp), lambda l, h, i: (l, h, i, 0)),
        out_shape=jax.ShapeDtypeStruct((depth, 2, d, fp), BF16),
        compiler_params=_params("parallel", "parallel", "parallel"),
        name="prep_w_in",
    )(w_in)
    tc = min(d, PREP_ROWS)
    w_o = pl.pallas_call(
        _cast_pad_rows_kernel,
        grid=(depth, d // tc),
        in_specs=[pl.BlockSpec((None, d_ff, tc), lambda l, i: (l, 0, i))],
        out_specs=pl.BlockSpec((None, fp, tc), lambda l, i: (l, 0, i)),
        out_shape=jax.ShapeDtypeStruct((depth, fp, d), BF16),
        compiler_params=_params("parallel", "parallel"),
        name="prep_w_out",
    )(w_out)
    return w_ab, w_o


def _mixer_weights(w, d_hgrn):
    col = lax.broadcasted_iota(jnp.int32, (1, 1, w.shape[2]), 2)
    lo = D_FOURIER + 2 * d_hgrn
    scale = jnp.where((col >= lo) & (col < lo + 2 * d_hgrn), 0.5, 1.0)
    return (w * scale).astype(BF16)


def kernel(x, c, w_ada, b_ada, w_ffn1_in, w_ffn1_out, ln1_g, ln1_b, w_in, lower_bounds, fourier_g, hgrn_g,
           w_out, ln2_g, ln2_b, w_ffn2_in, w_ffn2_out, ln3_g, ln3_b):
    bsz, s, d = x.shape
    depth = w_ada.shape[0]
    d_hgrn = hgrn_g.shape[1]
    d_in = w_in.shape[2]
    n_heads = d_hgrn // LANES
    assert fourier_g.shape[1] == D_FOURIER and d_in == D_FOURIER + 5 * d_hgrn
    alpha = float((2 * depth) ** 0.25)

    tm = min(s, 1024)
    tm_out = min(s, 512)
    tn_proj = next(t for t in (2048, 1024, 512) if d_in % t == 0)
    assert (2 * d_hgrn) % (PROJ_SUB_SLABS * LANES) == 0 and D_FOURIER == PROJ_SUB_SLABS * LANES
    assert s % (GLA_GROUP * CHUNK) == 0
    nb = 2 if bsz % 2 == 0 else 1
    tk_dft = min(2 * s, 2048)

    ada = _ada_proj(c, w_ada, b_ada).reshape(depth, bsz, N_MOD, d)

    dft_mat = _seq_dft_matrix(s)
    cos_c, sin_c = _dft_tables(LANES, LANES)
    w_chan = jnp.concatenate([cos_c, -sin_c], axis=1).astype(BF16)
    dft_scale = float((s * LANES) ** -0.5)

    w_ab1, w_o1 = _prep_ffn_weights(w_ffn1_in, w_ffn1_out)
    w_ab2, w_o2 = _prep_ffn_weights(w_ffn2_in, w_ffn2_out)
    w_mix_in = _mixer_weights(w_in, d_hgrn)
    w_mix_out = w_out.astype(BF16)

    for l in range(depth):
        ada_l = ada[l]
        x = _ffn(x, ada_l, w_ab1, w_o1, ln1_g[l][None], ln1_b[l][None],
                 layer=l, mod0=0, alpha=alpha, tm=tm, tf=FF_ALIGN)

        proj, ab = _in_proj(x, ada_l, w_mix_in, w_chan,
                            layer=l, mod0=3, tm=tm, tn=tn_proj, n_heads=n_heads)
        yf = _seq_dft(dft_mat, ab.reshape(2 * s, bsz * D_FOURIER), fourier_g[l][None],
                      bsz=bsz, scale=dft_scale, tm=tm, nb=nb, tk=tk_dft)
        yh = _gla(proj, lower_bounds, hgrn_g[l][None], layer=l, n_heads=n_heads, group=GLA_GROUP)
        x = _out_proj(yf, yh, x, ada_l, w_mix_out, ln2_g[l][None], ln2_b[l][None],
                      layer=l, mod0=3, alpha=alpha, tm=tm_out)

        x = _ffn(x, ada_l, w_ab2, w_o2, ln3_g[l][None], ln3_b[l][None],
                 layer=l, mod0=6, alpha=alpha, tm=tm, tf=FF_ALIGN)
    return x
```

```python
import functools

import jax
import jax.numpy as jnp
from jax import lax
from jax.experimental import pallas as pl
from jax.experimental.pallas import tpu as pltpu

F32 = jnp.float32
BF16 = jnp.bfloat16

LN_EPS = 1e-5
RMS_EPS = 1e-6
LOG2_E = 1.4426950408889634
CHUNK = 64
LANES = 128
N_MOD = 9
FOURIER_GROUPS = 4
D_FOURIER = FOURIER_GROUPS * LANES
FF_ALIGN = 512
PREP_ROWS = 256
FFN_COL_CHUNK = 1024
FFN_ROW_CHUNK = 256
GLA_GROUP = 16
CUMSUM_ROWS = 256
GLA_OUT_GROUP = 32
PROJ_SUB_SLABS = 4
VMEM_LIMIT = 60 * 1024 * 1024


def _params(*sem):
    return pltpu.CompilerParams(dimension_semantics=sem, vmem_limit_bytes=VMEM_LIMIT)


def _silu(z):
    return z * (0.5 * jnp.tanh(0.5 * z) + 0.5)


def _layer_norm(xf, g, b):
    mu = jnp.mean(xf, axis=-1, keepdims=True)
    xc = xf - mu
    var = jnp.mean(xc * xc, axis=-1, keepdims=True)
    return xc * lax.rsqrt(var + LN_EPS) * g + b


def _x_lookahead_map(n_batch, n_tiles, n_steps):
    def index_map(b, i, j):
        ahead = (j >= (n_steps + 1) // 2).astype(jnp.int32)
        t = jnp.minimum(b * n_tiles + i + ahead, n_batch * n_tiles - 1)
        return (t // n_tiles, t % n_tiles, 0)
    return index_map


def _dot(a, b):
    return jnp.dot(a, b, preferred_element_type=F32)


def _dot_nt(a, b):
    return lax.dot_general(a, b, (((1,), (1,)), ((), ())), preferred_element_type=F32)


def _ada_kernel(c_ref, w_ref, b_ref, o_ref):
    c = c_ref[...]
    o_ref[0] = _dot(_silu(c).astype(BF16), w_ref[0].astype(BF16)) + b_ref[0]


def _ada_proj(c, w_ada, b_ada):
    depth, d, n = w_ada.shape
    bsz = c.shape[0]
    tn = min(d, 1024)
    assert n % tn == 0
    return pl.pallas_call(
        _ada_kernel,
        grid=(depth, n // tn),
        in_specs=[
            pl.BlockSpec((bsz, d), lambda l, j: (0, 0)),
            pl.BlockSpec((1, d, tn), lambda l, j: (l, 0, j)),
            pl.BlockSpec((1, 1, tn), lambda l, j: (l, 0, j)),
        ],
        out_specs=pl.BlockSpec((1, bsz, tn), lambda l, j: (l, 0, j)),
        out_shape=jax.ShapeDtypeStruct((depth, bsz, n), F32),
        compiler_params=_params("parallel", "parallel"),
        name="ada_proj",
    )(c, w_ada, b_ada.reshape(depth, 1, n))


def _ffn_kernel(x_ref, ada_ref, wab_ref, wo_ref, g_ref, b_ref, o_ref, h_ref, hid_ref, *, mod0, alpha):
    j = pl.program_id(2)
    n = pl.num_programs(2) - 1
    tm, d = o_ref.shape

    def up(slot):
        h = h_ref[...]
        a = _dot(h, wab_ref[0])
        b = _dot(h, wab_ref[1])
        hid_ref[slot] = (_silu(a) * b).astype(BF16)

    @pl.when(j == 0)
    def _():
        shift = ada_ref[mod0:mod0 + 1, :]
        scale = ada_ref[mod0 + 1:mod0 + 2, :]
        x = x_ref[...]
        h_ref[...] = (x * (1.0 + scale) + shift).astype(BF16)
        o_ref[...] = alpha * x
        up(0)

    coef = 0.5 * (1.0 + ada_ref[mod0 + 2:mod0 + 3, :])

    @pl.when((j > 0) & (j < n))
    def _():
        prev = hid_ref[(j - 1) % 2]
        up(j % 2)
        cw = min(d, FFN_COL_CHUNK)
        for c in range(0, d, cw):
            o_ref[:, c:c + cw] += coef[:, c:c + cw] * _dot(prev, wo_ref[:, c:c + cw])

    @pl.when(j == n)
    def _():
        rc = min(tm, FFN_ROW_CHUNK)
        for r in range(0, tm, rc):
            y = o_ref[r:r + rc, :] + coef * _dot(hid_ref[(j - 1) % 2, r:r + rc, :], wo_ref[...])
            o_ref[r:r + rc, :] = _layer_norm(y, g_ref[...], b_ref[...])


def _ffn(x, ada_l, w_ab, w_out, ln_g, ln_b, *, layer, mod0, alpha, tm, tf):
    bsz, s, d = x.shape
    n = w_out.shape[1] // tf
    kern = functools.partial(_ffn_kernel, mod0=mod0, alpha=alpha)
    return pl.pallas_call(
        kern,
        grid=(bsz, s // tm, n + 1),
        in_specs=[
            pl.BlockSpec((None, tm, d), _x_lookahead_map(bsz, s // tm, n + 1)),
            pl.BlockSpec((None, N_MOD, d), lambda b, i, j: (b, 0, 0)),
            pl.BlockSpec((None, 2, d, tf), lambda b, i, j: (layer, 0, 0, jnp.minimum(j, n - 1))),
            pl.BlockSpec((None, tf, d), lambda b, i, j: (layer, jnp.maximum(j - 1, 0), 0)),
            pl.BlockSpec((1, d), lambda b, i, j: (0, 0)),
            pl.BlockSpec((1, d), lambda b, i, j: (0, 0)),
        ],
        out_specs=pl.BlockSpec((None, tm, d), lambda b, i, j: (b, i, 0)),
        out_shape=jax.ShapeDtypeStruct((bsz, s, d), F32),
        scratch_shapes=[pltpu.VMEM((tm, d), BF16), pltpu.VMEM((2, tm, tf), BF16)],
        compiler_params=_params("parallel", "parallel", "arbitrary"),
        name="ffn",
    )(x, ada_l, w_ab, w_out, ln_g, ln_b)


def _in_proj_kernel(x_ref, ada_ref, w_ref, wc_ref, o_ref, ab_ref, h_ref, *, mod0, n_heads):
    j = pl.program_id(2)
    q_lo, q_hi, gate_lo = FOURIER_GROUPS, FOURIER_GROUPS + n_heads, FOURIER_GROUPS + 4 * n_heads

    @pl.when(j == 0)
    def _():
        shift = ada_ref[mod0:mod0 + 1, :]
        scale = ada_ref[mod0 + 1:mod0 + 2, :]
        h_ref[...] = (x_ref[...] * (1.0 + scale) + shift).astype(BF16)

    n_slabs = o_ref.shape[0]
    sub = min(n_slabs, PROJ_SUB_SLABS)
    for c0 in range(0, n_slabs, sub):
        res = _dot(h_ref[...], w_ref[:, c0 * LANES:(c0 + sub) * LANES])
        slab = j * n_slabs + c0
        is_silu = ((slab >= q_lo) & (slab < q_hi)) | (slab >= gate_lo)
        res = jnp.where(is_silu, _silu(res), res)
        for c in range(sub):
            o_ref[c0 + c] = res[:, c * LANES:(c + 1) * LANES].astype(BF16)

    @pl.when(j == 0)
    def _():
        for g in range(FOURIER_GROUPS):
            u = o_ref[g]
            cs = _dot(u, wc_ref[...])
            ab_ref[0, :, g * LANES:(g + 1) * LANES] = cs[:, :LANES].astype(BF16)
            ab_ref[1, :, g * LANES:(g + 1) * LANES] = cs[:, LANES:].astype(BF16)


def _in_proj(x, ada_l, w, w_chan, *, layer, mod0, tm, tn, n_heads):
    bsz, s, d = x.shape
    n = w.shape[2]
    assert n_heads % PROJ_SUB_SLABS == 0 and FOURIER_GROUPS == PROJ_SUB_SLABS
    kern = functools.partial(_in_proj_kernel, mod0=mod0, n_heads=n_heads)
    return pl.pallas_call(
        kern,
        grid=(bsz, s // tm, n // tn),
        in_specs=[
            pl.BlockSpec((None, tm, d), _x_lookahead_map(bsz, s // tm, n // tn)),
            pl.BlockSpec((None, N_MOD, d), lambda b, i, j: (b, 0, 0)),
            pl.BlockSpec((None, d, tn), lambda b, i, j: (layer, 0, j)),
            pl.BlockSpec((LANES, 2 * LANES), lambda b, i, j: (0, 0)),
        ],
        out_specs=[
            pl.BlockSpec((None, tn // LANES, tm, LANES), lambda b, i, j: (b, j, i, 0)),
            pl.BlockSpec((2, tm, D_FOURIER), lambda b, i, j: (0, i, b)),
        ],
        out_shape=[
            jax.ShapeDtypeStruct((bsz, n // LANES, s, LANES), BF16),
            jax.ShapeDtypeStruct((2, s, bsz * D_FOURIER), BF16),
        ],
        scratch_shapes=[pltpu.VMEM((tm, d), BF16)],
        compiler_params=_params("parallel", "parallel", "arbitrary"),
        name="in_proj",
    )(x, ada_l, w, w_chan)


def _seq_dft_kernel(m_ref, ab_ref, g_ref, o_ref, acc_ref, *, scale, nb):
    k = pl.program_id(2)

    @pl.when(k == 0)
    def _():
        acc_ref[...] = jnp.zeros_like(acc_ref)

    acc_ref[...] += _dot(m_ref[...], ab_ref[...])

    @pl.when(k == pl.num_programs(2) - 1)
    def _():
        for bb in range(nb):
            for g in range(FOURIER_GROUPS):
                c0 = (bb * FOURIER_GROUPS + g) * LANES
                y = acc_ref[:, c0:c0 + LANES] * scale
                y = y * lax.rsqrt(jnp.mean(y * y, axis=-1, keepdims=True) + RMS_EPS)
                o_ref[bb, g] = (y * g_ref[:, g * LANES:(g + 1) * LANES]).astype(BF16)


def _seq_dft(dft_mat, ab, gain, *, bsz, scale, tm, nb, tk):
    s = dft_mat.shape[0]
    k_total = dft_mat.shape[1]
    tn = nb * D_FOURIER
    kern = functools.partial(_seq_dft_kernel, scale=scale, nb=nb)
    return pl.pallas_call(
        kern,
        grid=(bsz // nb, s // tm, k_total // tk),
        in_specs=[
            pl.BlockSpec((tm, tk), lambda n, i, k: (i, k)),
            pl.BlockSpec((tk, tn), lambda n, i, k: (k, n)),
            pl.BlockSpec((1, D_FOURIER), lambda n, i, k: (0, 0)),
        ],
        out_specs=pl.BlockSpec((nb, FOURIER_GROUPS, tm, LANES), lambda n, i, k: (n, 0, i, 0)),
        out_shape=jax.ShapeDtypeStruct((bsz, FOURIER_GROUPS, s, LANES), BF16),
        scratch_shapes=[pltpu.VMEM((tm, tn), F32)],
        compiler_params=_params("parallel", "parallel", "arbitrary"),
        name="seq_dft",
    )(dft_mat, ab, gain)


def _tri_cumsum(tri, x):
    x16 = x.astype(BF16)
    n = tri.shape[0]
    return jnp.concatenate([_dot(tri, x16[r:r + n, :]) for r in range(0, x.shape[0], n)], axis=0)


def _gla_kernel(q_ref, gt_ref, v_ref, zf_ref, zb_ref, lb_ref, gain_ref, tl_ref, tu_ref, o_ref,
                p_ref, kv_ref, qst_ref, sall_ref, d_ref, *, layer, group, out_group):
    s = q_ref.shape[0]
    n_chunks = s // CHUNK
    rows_g = group * CHUNK
    n_groups = n_chunks // group

    lb_raw = lb_ref[...]
    lb_exp = jnp.exp(lb_raw - jnp.max(lb_raw, axis=0, keepdims=True))
    lb_soft = lb_exp / jnp.sum(lb_exp, axis=0, keepdims=True)
    lbs = jnp.sum(lb_soft[:layer + 1], axis=0) - lb_soft[0]
    lb_f = lbs[0:1, :]
    lb_b = lbs[1:2, :]

    row = lax.broadcasted_iota(jnp.int32, (CHUNK, CHUNK), 0)
    col = lax.broadcasted_iota(jnp.int32, (CHUNK, CHUNK), 1)
    lower = row >= col
    upper = row <= col

    def gate_terms(zh_ref, rows, lb):
        half_span = 0.5 * (1.0 - lb)
        t = half_span * jnp.tanh(zh_ref[rows, :].astype(F32))
        return half_span - t, jnp.log((lb + half_span) + t)

    def scaled(q3, k3, b3, i_ref, i_last):
        b_ref = b3[:, i_ref:i_ref + 1, :]
        b_last = b3[:, i_last:i_last + 1, :]
        q_in = q3 * jnp.exp2(b3 - b_ref)
        k_in = k3 * jnp.exp2(b_ref - b3)
        q_st = q_in * jnp.exp2(b_ref)
        k_st = k_in * jnp.exp2(b_last - b_ref)
        return q_in, k_in, q_st, k_st, jnp.exp2(b_last)

    def pass0(gi, carry):
        r0 = pl.multiple_of(gi * rows_g, rows_g)
        rows = pl.ds(r0, rows_g)
        q3 = q_ref[rows, :].astype(F32).reshape(group, CHUNK, LANES)
        v = v_ref[rows, :]
        k_f, logf_f = gate_terms(zf_ref, rows, lb_f)
        k_b, logf_b = gate_terms(zb_ref, rows, lb_b)
        b_f = _tri_cumsum(tl_ref[...], logf_f) * LOG2_E
        b_b = _tri_cumsum(tu_ref[...], logf_b) * LOG2_E
        shape3 = (group, CHUNK, LANES)
        qf_in, kf_in, qf_st, kf_st, d_f = scaled(q3, k_f.reshape(shape3), b_f.reshape(shape3),
                                                 CHUNK // 2, CHUNK - 1)
        qb_in, kb_in, qb_st, kb_st, d_b = scaled(q3, k_b.reshape(shape3), b_b.reshape(shape3),
                                                 CHUNK - 1 - CHUNK // 2, 0)
        qst = jnp.concatenate([qf_st.reshape(rows_g, LANES), qb_st.reshape(rows_g, LANES)], axis=1)
        qst_ref[rows, :] = qst.astype(BF16)
        for g in range(group):
            c = gi * group + g
            s_f = _dot_nt(qf_in[g].astype(BF16), kf_in[g].astype(BF16))
            s_b = _dot_nt(qb_in[g].astype(BF16), kb_in[g].astype(BF16))
            p_ref[c] = (jnp.where(lower, s_f, 0.0) + jnp.where(upper, s_b, 0.0)).astype(BF16)
            k_st = jnp.concatenate([kf_st[g], kb_st[g]], axis=1).astype(BF16)
            v_t = v[g * CHUNK:(g + 1) * CHUNK, :].T
            kv_ref[c] = _dot(v_t, k_st)
            d_ref[c] = jnp.concatenate([d_f[g], d_b[g]], axis=1)
        return carry

    lax.fori_loop(0, n_groups, pass0, 0)

    def pass1(i, carry):
        st_f, st_b = carry
        cf = i
        cb = n_chunks - 1 - i
        sall_ref[cf, :, 0:LANES] = st_f.astype(BF16)
        sall_ref[cb, :, LANES:2 * LANES] = st_b.astype(BF16)
        st_f = d_ref[cf, :, 0:LANES] * st_f + kv_ref[cf, :, 0:LANES]
        st_b = d_ref[cb, :, LANES:2 * LANES] * st_b + kv_ref[cb, :, LANES:2 * LANES]
        return st_f, st_b

    zero = jnp.zeros((LANES, LANES), F32)
    lax.fori_loop(0, n_chunks, pass1, (zero, zero), unroll=2)

    gain = gain_ref[...]

    rows_o = out_group * CHUNK

    def pass2(gi, carry):
        r0 = pl.multiple_of(gi * rows_o, rows_o)
        rows = pl.ds(r0, rows_o)
        outs = []
        for g in range(out_group):
            c = gi * out_group + g
            rc = pl.ds(pl.multiple_of(r0 + g * CHUNK, CHUNK), CHUNK)
            o = _dot(p_ref[c], v_ref[rc, :]) + _dot_nt(qst_ref[rc, :], sall_ref[c])
            outs.append(o)
        o = jnp.concatenate(outs, axis=0)
        y = o * lax.rsqrt(jnp.mean(o * o, axis=-1, keepdims=True) + RMS_EPS) * gain
        o_ref[rows, :] = (y * gt_ref[rows, :].astype(F32)).astype(BF16)
        return carry

    lax.fori_loop(0, n_chunks // out_group, pass2, 0)


def _block_tri(n, upper):
    r = lax.broadcasted_iota(jnp.int32, (n, n), 0)
    c = lax.broadcasted_iota(jnp.int32, (n, n), 1)
    same = (r // CHUNK) == (c // CHUNK)
    tri = (r <= c) if upper else (r >= c)
    return (same & tri).astype(BF16)


def _gla(proj, lower_bounds, hgrn_g_l, *, layer, n_heads, group):
    bsz, _, s, _ = proj.shape
    depth = lower_bounds.shape[0]
    n_chunks = s // CHUNK
    rows_t = min(group * CHUNK, CUMSUM_ROWS)

    def slab(j):
        return pl.BlockSpec((None, None, s, LANES), lambda b, h: (b, FOURIER_GROUPS + j * n_heads + h, 0, 0))

    out_group = GLA_OUT_GROUP if n_chunks % GLA_OUT_GROUP == 0 else group
    kern = functools.partial(_gla_kernel, layer=layer, group=group, out_group=out_group)
    return pl.pallas_call(
        kern,
        grid=(bsz, n_heads),
        in_specs=[
            slab(0), slab(4), slab(1), slab(2), slab(3),
            pl.BlockSpec((depth, 2, LANES), lambda b, h: (0, 0, h)),
            pl.BlockSpec((1, LANES), lambda b, h: (0, h)),
            pl.BlockSpec((rows_t, rows_t), lambda b, h: (0, 0)),
            pl.BlockSpec((rows_t, rows_t), lambda b, h: (0, 0)),
        ],
        out_specs=pl.BlockSpec((None, None, s, LANES), lambda b, h: (b, h, 0, 0)),
        out_shape=jax.ShapeDtypeStruct((bsz, n_heads, s, LANES), BF16),
        scratch_shapes=[
            pltpu.VMEM((n_chunks, CHUNK, CHUNK), BF16),
            pltpu.VMEM((n_chunks, LANES, 2 * LANES), F32),
            pltpu.VMEM((s, 2 * LANES), BF16),
            pltpu.VMEM((n_chunks, LANES, 2 * LANES), BF16),
            pltpu.VMEM((n_chunks, 1, 2 * LANES), F32),
        ],
        compiler_params=_params("parallel", "parallel"),
        name="gla",
    )(proj, proj, proj, proj, proj, lower_bounds, hgrn_g_l,
      _block_tri(rows_t, False), _block_tri(rows_t, True))


def _out_proj_kernel(yf_ref, yh_ref, x_ref, ada_ref, w_ref, g_ref, b_ref, o_ref, *, mod0, alpha):
    coef = 1.0 + ada_ref[mod0 + 2:mod0 + 3, :]
    tm = o_ref.shape[0]
    rc = min(tm, FFN_ROW_CHUNK)
    for r in range(0, tm, rc):
        parts = ([yf_ref[g, r:r + rc, :] for g in range(yf_ref.shape[0])]
                 + [yh_ref[h, r:r + rc, :] for h in range(yh_ref.shape[0])])
        y = _dot(jnp.concatenate(parts, axis=-1), w_ref[...])
        o_ref[r:r + rc, :] = _layer_norm(alpha * x_ref[r:r + rc, :] + coef * y, g_ref[...], b_ref[...])


def _out_proj(yf, yh, x, ada_l, w, ln_g, ln_b, *, layer, mod0, alpha, tm):
    bsz, s, d = x.shape
    n_heads = yh.shape[1]
    kern = functools.partial(_out_proj_kernel, mod0=mod0, alpha=alpha)
    return pl.pallas_call(
        kern,
        grid=(bsz, s // tm),
        in_specs=[
            pl.BlockSpec((None, FOURIER_GROUPS, tm, LANES), lambda b, i: (b, 0, i, 0)),
            pl.BlockSpec((None, n_heads, tm, LANES), lambda b, i: (b, 0, i, 0)),
            pl.BlockSpec((None, tm, d), lambda b, i: (b, i, 0)),
            pl.BlockSpec((None, N_MOD, d), lambda b, i: (b, 0, 0)),
            pl.BlockSpec((None,) + w.shape[1:], lambda b, i: (layer, 0, 0)),
            pl.BlockSpec((1, d), lambda b, i: (0, 0)),
            pl.BlockSpec((1, d), lambda b, i: (0, 0)),
        ],
        out_specs=pl.BlockSpec((None, tm, d), lambda b, i: (b, i, 0)),
        out_shape=jax.ShapeDtypeStruct((bsz, s, d), F32),
        compiler_params=_params("parallel", "parallel"),
        name="out_proj",
    )(yf, yh, x, ada_l, w, ln_g, ln_b)


def _dft_tables(s, n_cols, stride=1):
    k = lax.broadcasted_iota(jnp.int32, (s, n_cols), 0)
    n = lax.broadcasted_iota(jnp.int32, (s, n_cols), 1)
    ang = ((k * (n * stride)) % s).astype(F32) * (2.0 * jnp.pi / s)
    return jnp.cos(ang), jnp.sin(ang)


def _seq_dft_matrix_kernel(ac_ref, as_ref, bc_ref, bs_ref, o_ref):
    s = o_ref.shape[1] // 2
    b_c = bc_ref[...]
    b_s = bs_ref[...]
    for n1 in range(s // LANES):
        a_c = ac_ref[:, n1:n1 + 1]
        a_s = as_ref[:, n1:n1 + 1]
        o_ref[:, n1 * LANES:(n1 + 1) * LANES] = (a_c * b_c - a_s * b_s).astype(BF16)
        o_ref[:, s + n1 * LANES:s + (n1 + 1) * LANES] = (a_s * b_c + a_c * b_s).astype(BF16)


def _seq_dft_matrix(s):
    n_coarse = s // LANES
    a_c, a_s = _dft_tables(s, n_coarse, stride=LANES)
    b_c, b_s = _dft_tables(s, LANES)
    tm = min(s, PREP_ROWS)
    coarse = pl.BlockSpec((tm, n_coarse), lambda i: (i, 0))
    fine = pl.BlockSpec((tm, LANES), lambda i: (i, 0))
    return pl.pallas_call(
        _seq_dft_matrix_kernel,
        grid=(s // tm,),
        in_specs=[coarse, coarse, fine, fine],
        out_specs=pl.BlockSpec((tm, 2 * s), lambda i: (i, 0)),
        out_shape=jax.ShapeDtypeStruct((s, 2 * s), BF16),
        compiler_params=_params("parallel"),
        name="seq_dft_matrix",
    )(a_c, a_s, b_c, b_s)


def _cast_pad_cols_kernel(w_ref, o_ref):
    n = w_ref.shape[-1]
    o_ref[:, :n] = w_ref[...].astype(BF16)
    if o_ref.shape[-1] > n:
        o_ref[:, n:] = jnp.zeros((o_ref.shape[0], o_ref.shape[-1] - n), BF16)


def _cast_pad_rows_kernel(w_ref, o_ref):
    n = w_ref.shape[0]
    o_ref[:n, :] = w_ref[...].astype(BF16)
    if o_ref.shape[0] > n:
        o_ref[n:, :] = jnp.zeros((o_ref.shape[0] - n, o_ref.shape[1]), BF16)


def _prep_ffn_weights(w_in, w_out):
    depth, d, two_ff = w_in.shape
    d_ff = two_ff // 2
    fp = -(-d_ff // FF_ALIGN) * FF_ALIGN
    assert d_ff % LANES == 0
    tr = min(d, PREP_ROWS)
    w_ab = pl.pallas_call(
        _cast_pad_cols_kernel,
        grid=(depth, 2, d // tr),
        in_specs=[pl.BlockSpec((None, tr, d_ff), lambda l, h, i: (l, i, h))],
        out_specs=pl.BlockSpec((None, None, tr, fp), lambda l, h, i: (l, h, i, 0)),
        out_shape=jax.ShapeDtypeStruct((depth, 2, d, fp), BF16),
        compiler_params=_params("parallel", "parallel", "parallel"),
        name="prep_w_in",
    )(w_in)
    tc = min(d, PREP_ROWS)
    w_o = pl.pallas_call(
        _cast_pad_rows_kernel,
        grid=(depth, d // tc),
        in_specs=[pl.BlockSpec((None, d_ff, tc), lambda l, i: (l, 0, i))],
        out_specs=pl.BlockSpec((None, fp, tc), lambda l, i: (l, 0, i)),
        out_shape=jax.ShapeDtypeStruct((depth, fp, d), BF16),
        compiler_params=_params("parallel", "parallel"),
        name="prep_w_out",
    )(w_out)
    return w_ab, w_o


def _mixer_weights(w, d_hgrn):
    col = lax.broadcasted_iota(jnp.int32, (1, 1, w.shape[2]), 2)
    lo = D_FOURIER + 2 * d_hgrn
    scale = jnp.where((col >= lo) & (col < lo + 2 * d_hgrn), 0.5, 1.0)
    return (w * scale).astype(BF16)


def kernel(x, c, w_ada, b_ada, w_ffn1_in, w_ffn1_out, ln1_g, ln1_b, w_in, lower_bounds, fourier_g, hgrn_g,
           w_out, ln2_g, ln2_b, w_ffn2_in, w_ffn2_out, ln3_g, ln3_b):
    bsz, s, d = x.shape
    depth = w_ada.shape[0]
    d_hgrn = hgrn_g.shape[1]
    d_in = w_in.shape[2]
    n_heads = d_hgrn // LANES
    assert fourier_g.shape[1] == D_FOURIER and d_in == D_FOURIER + 5 * d_hgrn
    alpha = float((2 * depth) ** 0.25)

    tm = min(s, 1024)
    tm_out = min(s, 512)
    tn_proj = next(t for t in (2048, 1024, 512) if d_in % t == 0)
    assert (2 * d_hgrn) % (PROJ_SUB_SLABS * LANES) == 0 and D_FOURIER == PROJ_SUB_SLABS * LANES
    assert s % (GLA_GROUP * CHUNK) == 0
    nb = 2 if bsz % 2 == 0 else 1
    tk_dft = min(2 * s, 2048)

    ada = _ada_proj(c, w_ada, b_ada).reshape(depth, bsz, N_MOD, d)

    dft_mat = _seq_dft_matrix(s)
    cos_c, sin_c = _dft_tables(LANES, LANES)
    w_chan = jnp.concatenate([cos_c, -sin_c], axis=1).astype(BF16)
    dft_scale = float((s * LANES) ** -0.5)

    w_ab1, w_o1 = _prep_ffn_weights(w_ffn1_in, w_ffn1_out)
    w_ab2, w_o2 = _prep_ffn_weights(w_ffn2_in, w_ffn2_out)
    w_mix_in = _mixer_weights(w_in, d_hgrn)
    w_mix_out = w_out.astype(BF16)

    for l in range(depth):
        ada_l = ada[l]
        x = _ffn(x, ada_l, w_ab1, w_o1, ln1_g[l][None], ln1_b[l][None],
                 layer=l, mod0=0, alpha=alpha, tm=tm, tf=FF_ALIGN)

        proj, ab = _in_proj(x, ada_l, w_mix_in, w_chan,
                            layer=l, mod0=3, tm=tm, tn=tn_proj, n_heads=n_heads)
        yf = _seq_dft(dft_mat, ab.reshape(2 * s, bsz * D_FOURIER), fourier_g[l][None],
                      bsz=bsz, scale=dft_scale, tm=tm, nb=nb, tk=tk_dft)
        yh = _gla(proj, lower_bounds, hgrn_g[l][None], layer=l, n_heads=n_heads, group=GLA_GROUP)
        x = _out_proj(yf, yh, x, ada_l, w_mix_out, ln2_g[l][None], ln2_b[l][None],
                      layer=l, mod0=3, alpha=alpha, tm=tm_out)

        x = _ffn(x, ada_l, w_ab2, w_o2, ln3_g[l][None], ln3_b[l][None],
                 layer=l, mod0=6, alpha=alpha, tm=tm, tf=FF_ALIGN)
    return x
```

```python
import functools

import jax
import jax.numpy as jnp
from jax import lax
from jax.experimental import pallas as pl
from jax.experimental.pallas import tpu as pltpu

F32 = jnp.float32
BF16 = jnp.bfloat16

LN_EPS = 1e-5
RMS_EPS = 1e-6
LOG2_E = 1.4426950408889634
CHUNK = 64
LANES = 128
N_MOD = 9
FOURIER_GROUPS = 4
D_FOURIER = FOURIER_GROUPS * LANES
FF_ALIGN = 512
PREP_ROWS = 256
FFN_COL_CHUNK = 1024
FFN_ROW_CHUNK = 256
GLA_GROUP = 16
CUMSUM_ROWS = 256
GLA_OUT_GROUP = 32
PROJ_SUB_SLABS = 4
VMEM_LIMIT = 60 * 1024 * 1024


def _params(*sem):
    return pltpu.CompilerParams(dimension_semantics=sem, vmem_limit_bytes=VMEM_LIMIT)


def _silu(z):
    return z * (0.5 * jnp.tanh(0.5 * z) + 0.5)


def _layer_norm(xf, g, b):
    mu = jnp.mean(xf, axis=-1, keepdims=True)
    xc = xf - mu
    var = jnp.mean(xc * xc, axis=-1, keepdims=True)
    return xc * lax.rsqrt(var + LN_EPS) * g + b


def _x_lookahead_map(n_batch, n_tiles, n_steps):
    def index_map(b, i, j):
        ahead = (j >= (n_steps + 1) // 2).astype(jnp.int32)
        t = jnp.minimum(b * n_tiles + i + ahead, n_batch * n_tiles - 1)
        return (t // n_tiles, t % n_tiles, 0)
    return index_map


def _dot(a, b):
    return jnp.dot(a, b, preferred_element_type=F32)


def _dot_nt(a, b):
    return lax.dot_general(a, b, (((1,), (1,)), ((), ())), preferred_element_type=F32)


def _ada_kernel(c_ref, w_ref, b_ref, o_ref):
    c = c_ref[...]
    o_ref[0] = _dot(_silu(c).astype(BF16), w_ref[0].astype(BF16)) + b_ref[0]


def _ada_proj(c, w_ada, b_ada):
    depth, d, n = w_ada.shape
    bsz = c.shape[0]
    tn = min(d, 1024)
    assert n % tn == 0
    return pl.pallas_call(
        _ada_kernel,
        grid=(depth, n // tn),
        in_specs=[
            pl.BlockSpec((bsz, d), lambda l, j: (0, 0)),
            pl.BlockSpec((1, d, tn), lambda l, j: (l, 0, j)),
            pl.BlockSpec((1, 1, tn), lambda l, j: (l, 0, j)),
        ],
        out_specs=pl.BlockSpec((1, bsz, tn), lambda l, j: (l, 0, j)),
        out_shape=jax.ShapeDtypeStruct((depth, bsz, n), F32),
        compiler_params=_params("parallel", "parallel"),
        name="ada_proj",
    )(c, w_ada, b_ada.reshape(depth, 1, n))


def _ffn_kernel(x_ref, ada_ref, wab_ref, wo_ref, g_ref, b_ref, o_ref, h_ref, hid_ref, *, mod0, alpha):
    j = pl.program_id(2)
    n = pl.num_programs(2) - 1
    tm, d = o_ref.shape

    def up(slot):
        h = h_ref[...]
        a = _dot(h, wab_ref[0])
        b = _dot(h, wab_ref[1])
        hid_ref[slot] = (_silu(a) * b).astype(BF16)

    @pl.when(j == 0)
    def _():
        shift = ada_ref[mod0:mod0 + 1, :]
        scale = ada_ref[mod0 + 1:mod0 + 2, :]
        x = x_ref[...]
        h_ref[...] = (x * (1.0 + scale) + shift).astype(BF16)
        o_ref[...] = alpha * x
        up(0)

    coef = 0.5 * (1.0 + ada_ref[mod0 + 2:mod0 + 3, :])

    @pl.when((j > 0) & (j < n))
    def _():
        prev = hid_ref[(j - 1) % 2]
        up(j % 2)
        cw = min(d, FFN_COL_CHUNK)
        for c in range(0, d, cw):
            o_ref[:, c:c + cw] += coef[:, c:c + cw] * _dot(prev, wo_ref[:, c:c + cw])

    @pl.when(j == n)
    def _():
        rc = min(tm, FFN_ROW_CHUNK)
        for r in range(0, tm, rc):
            y = o_ref[r:r + rc, :] + coef * _dot(hid_ref[(j - 1) % 2, r:r + rc, :], wo_ref[...])
            o_ref[r:r + rc, :] = _layer_norm(y, g_ref[...], b_ref[...])


def _ffn(x, ada_l, w_ab, w_out, ln_g, ln_b, *, layer, mod0, alpha, tm, tf):
    bsz, s, d = x.shape
    n = w_out.shape[1] // tf
    kern = functools.partial(_ffn_kernel, mod0=mod0, alpha=alpha)
    return pl.pallas_call(
        kern,
        grid=(bsz, s // tm, n + 1),
        in_specs=[
            pl.BlockSpec((None, tm, d), _x_lookahead_map(bsz, s // tm, n + 1)),
            pl.BlockSpec((None, N_MOD, d), lambda b, i, j: (b, 0, 0)),
            pl.BlockSpec((None, 2, d, tf), lambda b, i, j: (layer, 0, 0, jnp.minimum(j, n - 1))),
            pl.BlockSpec((None, tf, d), lambda b, i, j: (layer, jnp.maximum(j - 1, 0), 0)),
            pl.BlockSpec((1, d), lambda b, i, j: (0, 0)),
            pl.BlockSpec((1, d), lambda b, i, j: (0, 0)),
        ],
        out_specs=pl.BlockSpec((None, tm, d), lambda b, i, j: (b, i, 0)),
        out_shape=jax.ShapeDtypeStruct((bsz, s, d), F32),
        scratch_shapes=[pltpu.VMEM((tm, d), BF16), pltpu.VMEM((2, tm, tf), BF16)],
        compiler_params=_params("parallel", "parallel", "arbitrary"),
        name="ffn",
    )(x, ada_l, w_ab, w_out, ln_g, ln_b)


def _in_proj_kernel(x_ref, ada_ref, w_ref, wc_ref, o_ref, ab_ref, h_ref, *, mod0, n_heads):
    j = pl.program_id(2)
    q_lo, q_hi, gate_lo = FOURIER_GROUPS, FOURIER_GROUPS + n_heads, FOURIER_GROUPS + 4 * n_heads

    @pl.when(j == 0)
    def _():
        shift = ada_ref[mod0:mod0 + 1, :]
        scale = ada_ref[mod0 + 1:mod0 + 2, :]
        h_ref[...] = (x_ref[...] * (1.0 + scale) + shift).astype(BF16)

    n_slabs = o_ref.shape[0]
    sub = min(n_slabs, PROJ_SUB_SLABS)
    for c0 in range(0, n_slabs, sub):
        res = _dot(h_ref[...], w_ref[:, c0 * LANES:(c0 + sub) * LANES])
        slab = j * n_slabs + c0
        is_silu = ((slab >= q_lo) & (slab < q_hi)) | (slab >= gate_lo)
        res = jnp.where(is_silu, _silu(res), res)
        for c in range(sub):
            o_ref[c0 + c] = res[:, c * LANES:(c + 1) * LANES].astype(BF16)

    @pl.when(j == 0)
    def _():
        for g in range(FOURIER_GROUPS):
            u = o_ref[g]
            cs = _dot(u, wc_ref[...])
            ab_ref[0, :, g * LANES:(g + 1) * LANES] = cs[:, :LANES].astype(BF16)
            ab_ref[1, :, g * LANES:(g + 1) * LANES] = cs[:, LANES:].astype(BF16)


def _in_proj(x, ada_l, w, w_chan, *, layer, mod0, tm, tn, n_heads):
    bsz, s, d = x.shape
    n = w.shape[2]
    assert n_heads % PROJ_SUB_SLABS == 0 and FOURIER_GROUPS == PROJ_SUB_SLABS
    kern = functools.partial(_in_proj_kernel, mod0=mod0, n_heads=n_heads)
    return pl.pallas_call(
        kern,
        grid=(bsz, s // tm, n // tn),
        in_specs=[
            pl.BlockSpec((None, tm, d), _x_lookahead_map(bsz, s // tm, n // tn)),
            pl.BlockSpec((None, N_MOD, d), lambda b, i, j: (b, 0, 0)),
            pl.BlockSpec((None, d, tn), lambda b, i, j: (layer, 0, j)),
            pl.BlockSpec((LANES, 2 * LANES), lambda b, i, j: (0, 0)),
        ],
        out_specs=[
            pl.BlockSpec((None, tn // LANES, tm, LANES), lambda b, i, j: (b, j, i, 0)),
            pl.BlockSpec((2, tm, D_FOURIER), lambda b, i, j: (0, i, b)),
        ],
        out_shape=[
            jax.ShapeDtypeStruct((bsz, n // LANES, s, LANES), BF16),
            jax.ShapeDtypeStruct((2, s, bsz * D_FOURIER), BF16),
        ],
        scratch_shapes=[pltpu.VMEM((tm, d), BF16)],
        compiler_params=_params("parallel", "parallel", "arbitrary"),
        name="in_proj",
    )(x, ada_l, w, w_chan)


def _seq_dft_kernel(m_ref, ab_ref, g_ref, o_ref, acc_ref, *, scale, nb):
    k = pl.program_id(2)

    @pl.when(k == 0)
    def _():
        acc_ref[...] = jnp.zeros_like(acc_ref)

    acc_ref[...] += _dot(m_ref[...], ab_ref[...])

    @pl.when(k == pl.num_programs(2) - 1)
    def _():
        for bb in range(nb):
            for g in range(FOURIER_GROUPS):
                c0 = (bb * FOURIER_GROUPS + g) * LANES
                y = acc_ref[:, c0:c0 + LANES] * scale
                y = y * lax.rsqrt(jnp.mean(y * y, axis=-1, keepdims=True) + RMS_EPS)
                o_ref[bb, g] = (y * g_ref[:, g * LANES:(g + 1) * LANES]).astype(BF16)


def _seq_dft(dft_mat, ab, gain, *, bsz, scale, tm, nb, tk):
    s = dft_mat.shape[0]
    k_total = dft_mat.shape[1]
    tn = nb * D_FOURIER
    kern = functools.partial(_seq_dft_kernel, scale=scale, nb=nb)
    return pl.pallas_call(
        kern,
        grid=(bsz // nb, s // tm, k_total // tk),
        in_specs=[
            pl.BlockSpec((tm, tk), lambda n, i, k: (i, k)),
            pl.BlockSpec((tk, tn), lambda n, i, k: (k, n)),
            pl.BlockSpec((1, D_FOURIER), lambda n, i, k: (0, 0)),
        ],
        out_specs=pl.BlockSpec((nb, FOURIER_GROUPS, tm, LANES), lambda n, i, k: (n, 0, i, 0)),
        out_shape=jax.ShapeDtypeStruct((bsz, FOURIER_GROUPS, s, LANES), BF16),
        scratch_shapes=[pltpu.VMEM((tm, tn), F32)],
        compiler_params=_params("parallel", "parallel", "arbitrary"),
        name="seq_dft",
    )(dft_mat, ab, gain)


def _tri_cumsum(tri, x):
    x16 = x.astype(BF16)
    n = tri.shape[0]
    return jnp.concatenate([_dot(tri, x16[r:r + n, :]) for r in range(0, x.shape[0], n)], axis=0)


def _gla_kernel(q_ref, gt_ref, v_ref, zf_ref, zb_ref, lb_ref, gain_ref, tl_ref, tu_ref, o_ref,
                p_ref, kv_ref, qst_ref, sall_ref, d_ref, *, layer, group, out_group):
    s = q_ref.shape[0]
    n_chunks = s // CHUNK
    rows_g = group * CHUNK
    n_groups = n_chunks // group

    lb_raw = lb_ref[...]
    lb_exp = jnp.exp(lb_raw - jnp.max(lb_raw, axis=0, keepdims=True))
    lb_soft = lb_exp / jnp.sum(lb_exp, axis=0, keepdims=True)
    lbs = jnp.sum(lb_soft[:layer + 1], axis=0) - lb_soft[0]
    lb_f = lbs[0:1, :]
    lb_b = lbs[1:2, :]

    row = lax.broadcasted_iota(jnp.int32, (CHUNK, CHUNK), 0)
    col = lax.broadcasted_iota(jnp.int32, (CHUNK, CHUNK), 1)
    lower = row >= col
    upper = row <= col

    def gate_terms(z_ref, rows, lb):
        sig = 1.0 / (1.0 + jnp.exp2(z_ref[rows, :].astype(F32) * (-LOG2_E)))
        f = lb + (1.0 - lb) * sig
        return 1.0 - f, jnp.log(f)

    def scaled(q3, k3, b3, i_ref, i_last):
        b_ref = b3[:, i_ref:i_ref + 1, :]
        b_last = b3[:, i_last:i_last + 1, :]
        q_in = q3 * jnp.exp2(b3 - b_ref)
        k_in = k3 * jnp.exp2(b_ref - b3)
        q_st = q_in * jnp.exp2(b_ref)
        k_st = k_in * jnp.exp2(b_last - b_ref)
        return q_in, k_in, q_st, k_st, jnp.exp2(b_last)

    def pass0(gi, carry):
        r0 = pl.multiple_of(gi * rows_g, rows_g)
        rows = pl.ds(r0, rows_g)
        q3 = q_ref[rows, :].astype(F32).reshape(group, CHUNK, LANES)
        v = v_ref[rows, :]
        k_f, logf_f = gate_terms(zf_ref, rows, lb_f)
        k_b, logf_b = gate_terms(zb_ref, rows, lb_b)
        b_f = _tri_cumsum(tl_ref[...], logf_f) * LOG2_E
        b_b = _tri_cumsum(tu_ref[...], logf_b) * LOG2_E
        shape3 = (group, CHUNK, LANES)
        qf_in, kf_in, qf_st, kf_st, d_f = scaled(q3, k_f.reshape(shape3), b_f.reshape(shape3),
                                                 CHUNK // 2, CHUNK - 1)
        qb_in, kb_in, qb_st, kb_st, d_b = scaled(q3, k_b.reshape(shape3), b_b.reshape(shape3),
                                                 CHUNK - 1 - CHUNK // 2, 0)
        qst = jnp.concatenate([qf_st.reshape(rows_g, LANES), qb_st.reshape(rows_g, LANES)], axis=1)
        qst_ref[rows, :] = qst.astype(BF16)
        for g in range(group):
            c = gi * group + g
            s_f = _dot_nt(qf_in[g].astype(BF16), kf_in[g].astype(BF16))
            s_b = _dot_nt(qb_in[g].astype(BF16), kb_in[g].astype(BF16))
            p_ref[c] = (jnp.where(lower, s_f, 0.0) + jnp.where(upper, s_b, 0.0)).astype(BF16)
            k_st = jnp.concatenate([kf_st[g], kb_st[g]], axis=1).astype(BF16)
            v_t = v[g * CHUNK:(g + 1) * CHUNK, :].T
            kv_ref[c] = _dot(v_t, k_st)
            d_ref[c] = jnp.concatenate([d_f[g], d_b[g]], axis=1)
        return carry

    lax.fori_loop(0, n_groups, pass0, 0)

    def pass1(i, carry):
        st_f, st_b = carry
        cf = i
        cb = n_chunks - 1 - i
        sall_ref[cf, :, 0:LANES] = st_f.astype(BF16)
        sall_ref[cb, :, LANES:2 * LANES] = st_b.astype(BF16)
        st_f = d_ref[cf, :, 0:LANES] * st_f + kv_ref[cf, :, 0:LANES]
        st_b = d_ref[cb, :, LANES:2 * LANES] * st_b + kv_ref[cb, :, LANES:2 * LANES]
        return st_f, st_b

    zero = jnp.zeros((LANES, LANES), F32)
    lax.fori_loop(0, n_chunks, pass1, (zero, zero), unroll=2)

    gain = gain_ref[...]

    rows_o = out_group * CHUNK

    def pass2(gi, carry):
        r0 = pl.multiple_of(gi * rows_o, rows_o)
        rows = pl.ds(r0, rows_o)
        outs = []
        for g in range(out_group):
            c = gi * out_group + g
            rc = pl.ds(pl.multiple_of(r0 + g * CHUNK, CHUNK), CHUNK)
            o = _dot(p_ref[c], v_ref[rc, :]) + _dot_nt(qst_ref[rc, :], sall_ref[c])
            outs.append(o)
        o = jnp.concatenate(outs, axis=0)
        y = o * lax.rsqrt(jnp.mean(o * o, axis=-1, keepdims=True) + RMS_EPS) * gain
        o_ref[rows, :] = (y * gt_ref[rows, :].astype(F32)).astype(BF16)
        return carry

    lax.fori_loop(0, n_chunks // out_group, pass2, 0)


def _block_tri(n, upper):
    r = lax.broadcasted_iota(jnp.int32, (n, n), 0)
    c = lax.broadcasted_iota(jnp.int32, (n, n), 1)
    same = (r // CHUNK) == (c // CHUNK)
    tri = (r <= c) if upper else (r >= c)
    return (same & tri).astype(BF16)


def _gla(proj, lower_bounds, hgrn_g_l, *, layer, n_heads, group):
    bsz, _, s, _ = proj.shape
    depth = lower_bounds.shape[0]
    n_chunks = s // CHUNK
    rows_t = min(group * CHUNK, CUMSUM_ROWS)

    def slab(j):
        return pl.BlockSpec((None, None, s, LANES), lambda b, h: (b, FOURIER_GROUPS + j * n_heads + h, 0, 0))

    out_group = GLA_OUT_GROUP if n_chunks % GLA_OUT_GROUP == 0 else group
    kern = functools.partial(_gla_kernel, layer=layer, group=group, out_group=out_group)
    return pl.pallas_call(
        kern,
        grid=(bsz, n_heads),
        in_specs=[
            slab(0), slab(4), slab(1), slab(2), slab(3),
            pl.BlockSpec((depth, 2, LANES), lambda b, h: (0, 0, h)),
            pl.BlockSpec((1, LANES), lambda b, h: (0, h)),
            pl.BlockSpec((rows_t, rows_t), lambda b, h: (0, 0)),
            pl.BlockSpec((rows_t, rows_t), lambda b, h: (0, 0)),
        ],
        out_specs=pl.BlockSpec((None, None, s, LANES), lambda b, h: (b, h, 0, 0)),
        out_shape=jax.ShapeDtypeStruct((bsz, n_heads, s, LANES), BF16),
        scratch_shapes=[
            pltpu.VMEM((n_chunks, CHUNK, CHUNK), BF16),
            pltpu.VMEM((n_chunks, LANES, 2 * LANES), F32),
            pltpu.VMEM((s, 2 * LANES), BF16),
            pltpu.VMEM((n_chunks, LANES, 2 * LANES), BF16),
            pltpu.VMEM((n_chunks, 1, 2 * LANES), F32),
        ],
        compiler_params=_params("parallel", "parallel"),
        name="gla",
    )(proj, proj, proj, proj, proj, lower_bounds, hgrn_g_l,
      _block_tri(rows_t, False), _block_tri(rows_t, True))


def _out_proj_kernel(yf_ref, yh_ref, x_ref, ada_ref, w_ref, g_ref, b_ref, o_ref, *, mod0, alpha):
    coef = 1.0 + ada_ref[mod0 + 2:mod0 + 3, :]
    tm = o_ref.shape[0]
    rc = min(tm, FFN_ROW_CHUNK)
    for r in range(0, tm, rc):
        parts = ([yf_ref[g, r:r + rc, :] for g in range(yf_ref.shape[0])]
                 + [yh_ref[h, r:r + rc, :] for h in range(yh_ref.shape[0])])
        y = _dot(jnp.concatenate(parts, axis=-1), w_ref[...])
        o_ref[r:r + rc, :] = _layer_norm(alpha * x_ref[r:r + rc, :] + coef * y, g_ref[...], b_ref[...])


def _out_proj(yf, yh, x, ada_l, w, ln_g, ln_b, *, layer, mod0, alpha, tm):
    bsz, s, d = x.shape
    n_heads = yh.shape[1]
    kern = functools.partial(_out_proj_kernel, mod0=mod0, alpha=alpha)
    return pl.pallas_call(
        kern,
        grid=(bsz, s // tm),
        in_specs=[
            pl.BlockSpec((None, FOURIER_GROUPS, tm, LANES), lambda b, i: (b, 0, i, 0)),
            pl.BlockSpec((None, n_heads, tm, LANES), lambda b, i: (b, 0, i, 0)),
            pl.BlockSpec((None, tm, d), lambda b, i: (b, i, 0)),
            pl.BlockSpec((None, N_MOD, d), lambda b, i: (b, 0, 0)),
            pl.BlockSpec((None,) + w.shape[1:], lambda b, i: (layer, 0, 0)),
            pl.BlockSpec((1, d), lambda b, i: (0, 0)),
            pl.BlockSpec((1, d), lambda b, i: (0, 0)),
        ],
        out_specs=pl.BlockSpec((None, tm, d), lambda b, i: (b, i, 0)),
        out_shape=jax.ShapeDtypeStruct((bsz, s, d), F32),
        compiler_params=_params("parallel", "parallel"),
        name="out_proj",
    )(yf, yh, x, ada_l, w, ln_g, ln_b)


def _dft_tables(s, n_cols, stride=1):
    k = lax.broadcasted_iota(jnp.int32, (s, n_cols), 0)
    n = lax.broadcasted_iota(jnp.int32, (s, n_cols), 1)
    ang = ((k * (n * stride)) % s).astype(F32) * (2.0 * jnp.pi / s)
    return jnp.cos(ang), jnp.sin(ang)


def _seq_dft_matrix_kernel(ac_ref, as_ref, bc_ref, bs_ref, o_ref):
    s = o_ref.shape[1] // 2
    b_c = bc_ref[...]
    b_s = bs_ref[...]
    for n1 in range(s // LANES):
        a_c = ac_ref[:, n1:n1 + 1]
        a_s = as_ref[:, n1:n1 + 1]
        o_ref[:, n1 * LANES:(n1 + 1) * LANES] = (a_c * b_c - a_s * b_s).astype(BF16)
        o_ref[:, s + n1 * LANES:s + (n1 + 1) * LANES] = (a_s * b_c + a_c * b_s).astype(BF16)


def _seq_dft_matrix(s):
    n_coarse = s // LANES
    a_c, a_s = _dft_tables(s, n_coarse, stride=LANES)
    b_c, b_s = _dft_tables(s, LANES)
    tm = min(s, PREP_ROWS)
    coarse = pl.BlockSpec((tm, n_coarse), lambda i: (i, 0))
    fine = pl.BlockSpec((tm, LANES), lambda i: (i, 0))
    return pl.pallas_call(
        _seq_dft_matrix_kernel,
        grid=(s // tm,),
        in_specs=[coarse, coarse, fine, fine],
        out_specs=pl.BlockSpec((tm, 2 * s), lambda i: (i, 0)),
        out_shape=jax.ShapeDtypeStruct((s, 2 * s), BF16),
        compiler_params=_params("parallel"),
        name="seq_dft_matrix",
    )(a_c, a_s, b_c, b_s)


def _cast_pad_cols_kernel(w_ref, o_ref):
    n = w_ref.shape[-1]
    o_ref[:, :n] = w_ref[...].astype(BF16)
    if o_ref.shape[-1] > n:
        o_ref[:, n:] = jnp.zeros((o_ref.shape[0], o_ref.shape[-1] - n), BF16)


def _cast_pad_rows_kernel(w_ref, o_ref):
    n = w_ref.shape[0]
    o_ref[:n, :] = w_ref[...].astype(BF16)
    if o_ref.shape[0] > n:
        o_ref[n:, :] = jnp.zeros((o_ref.shape[0] - n, o_ref.shape[1]), BF16)


def _prep_ffn_weights(w_in, w_out):
    depth, d, two_ff = w_in.shape
    d_ff = two_ff // 2
    fp = -(-d_ff // FF_ALIGN) * FF_ALIGN
    assert d_ff % LANES == 0
    tr = min(d, PREP_ROWS)
    w_ab = pl.pallas_call(
        _cast_pad_cols_kernel,
        grid=(depth, 2, d // tr),
        in_specs=[pl.BlockSpec((None, tr, d_ff), lambda l, h, i: (l, i, h))],
        out_specs=pl.BlockSpec((None, None, tr, fp), lambda l, h, i: (l, h, i, 0)),
        out_shape=jax.ShapeDtypeStruct((depth, 2, d, fp), BF16),
        compiler_params=_params("parallel", "parallel", "parallel"),
        name="prep_w_in",
    )(w_in)
    tc = min(d, PREP_ROWS)
    w_o = pl.pallas_call(
        _cast_pad_rows_kernel,
        grid=(depth, d // tc),
        in_specs=[pl.BlockSpec((None, d_ff, tc), lambda l, i: (l, 0, i))],
        out_specs=pl.BlockSpec((None, fp, tc), lambda l, i: (l, 0, i)),
        out_shape=jax.ShapeDtypeStruct((depth, fp, d), BF16),
        compiler_params=_params("parallel", "parallel"),
        name="prep_w_out",
    )(w_out)
    return w_ab, w_o


def kernel(x, c, w_ada, b_ada, w_ffn1_in, w_ffn1_out, ln1_g, ln1_b, w_in, lower_bounds, fourier_g, hgrn_g,
           w_out, ln2_g, ln2_b, w_ffn2_in, w_ffn2_out, ln3_g, ln3_b):
    bsz, s, d = x.shape
    depth = w_ada.shape[0]
    d_hgrn = hgrn_g.shape[1]
    d_in = w_in.shape[2]
    n_heads = d_hgrn // LANES
    assert fourier_g.shape[1] == D_FOURIER and d_in == D_FOURIER + 5 * d_hgrn
    alpha = float((2 * depth) ** 0.25)

    tm = min(s, 1024)
    tm_out = min(s, 512)
    tn_proj = next(t for t in (2048, 1024, 512) if d_in % t == 0)
    assert (2 * d_hgrn) % (PROJ_SUB_SLABS * LANES) == 0 and D_FOURIER == PROJ_SUB_SLABS * LANES
    assert s % (GLA_GROUP * CHUNK) == 0
    nb = 2 if bsz % 2 == 0 else 1
    tk_dft = min(2 * s, 2048)

    ada = _ada_proj(c, w_ada, b_ada).reshape(depth, bsz, N_MOD, d)

    dft_mat = _seq_dft_matrix(s)
    cos_c, sin_c = _dft_tables(LANES, LANES)
    w_chan = jnp.concatenate([cos_c, -sin_c], axis=1).astype(BF16)
    dft_scale = float((s * LANES) ** -0.5)

    w_ab1, w_o1 = _prep_ffn_weights(w_ffn1_in, w_ffn1_out)
    w_ab2, w_o2 = _prep_ffn_weights(w_ffn2_in, w_ffn2_out)
    w_mix_in = w_in.astype(BF16)
    w_mix_out = w_out.astype(BF16)

    for l in range(depth):
        ada_l = ada[l]
        x = _ffn(x, ada_l, w_ab1, w_o1, ln1_g[l][None], ln1_b[l][None],
                 layer=l, mod0=0, alpha=alpha, tm=tm, tf=FF_ALIGN)

        proj, ab = _in_proj(x, ada_l, w_mix_in, w_chan,
                            layer=l, mod0=3, tm=tm, tn=tn_proj, n_heads=n_heads)
        yf = _seq_dft(dft_mat, ab.reshape(2 * s, bsz * D_FOURIER), fourier_g[l][None],
                      bsz=bsz, scale=dft_scale, tm=tm, nb=nb, tk=tk_dft)
        yh = _gla(proj, lower_bounds, hgrn_g[l][None], layer=l, n_heads=n_heads, group=GLA_GROUP)
        x = _out_proj(yf, yh, x, ada_l, w_mix_out, ln2_g[l][None], ln2_b[l][None],
                      layer=l, mod0=3, alpha=alpha, tm=tm_out)

        x = _ffn(x, ada_l, w_ab2, w_o2, ln3_g[l][None], ln3_b[l][None],
                 layer=l, mod0=6, alpha=alpha, tm=tm, tf=FF_ALIGN)
    return x
```

```python
import functools

import jax
import jax.numpy as jnp
from jax import lax
from jax.experimental import pallas as pl
from jax.experimental.pallas import tpu as pltpu

F32 = jnp.float32
BF16 = jnp.bfloat16

LN_EPS = 1e-5
RMS_EPS = 1e-6
LOG2_E = 1.4426950408889634
N_MOD = 9
CHUNK = 64
FOURIER_GROUPS = 4

LANES = 128
MXU_TILE = 256
VMEM_LIMIT = 60 * 1024 * 1024

D_FOURIER = FOURIER_GROUPS * LANES
FF_ALIGN = 2 * MXU_TILE
PREP_ROWS = 256
FFN_COL_CHUNK = 1024
FFN_ROW_CHUNK = 256
GLA_GROUP = 16
CUMSUM_ROWS = MXU_TILE
GLA_OUT_GROUP = 32
PROJ_SUB_SLABS = 4


def _params(*sem):
    return pltpu.CompilerParams(dimension_semantics=sem, vmem_limit_bytes=VMEM_LIMIT)


def _silu(z):
    return z * (0.5 * jnp.tanh(0.5 * z) + 0.5)


def _layer_norm(xf, g, b):
    mu = jnp.mean(xf, axis=-1, keepdims=True)
    xc = xf - mu
    var = jnp.mean(xc * xc, axis=-1, keepdims=True)
    return xc * lax.rsqrt(var + LN_EPS) * g + b


def _x_lookahead_map(n_batch, n_tiles, n_steps):
    def index_map(b, i, j):
        ahead = (j >= (n_steps + 1) // 2).astype(jnp.int32)
        t = jnp.minimum(b * n_tiles + i + ahead, n_batch * n_tiles - 1)
        return (t // n_tiles, t % n_tiles, 0)
    return index_map


def _dot(a, b):
    return jnp.dot(a, b, preferred_element_type=F32)


def _dot_nt(a, b):
    return lax.dot_general(a, b, (((1,), (1,)), ((), ())), preferred_element_type=F32)


def _ada_kernel(c_ref, w_ref, b_ref, o_ref):
    c = c_ref[...]
    o_ref[0] = _dot(_silu(c).astype(BF16), w_ref[0].astype(BF16)) + b_ref[0]


def _ada_proj(c, w_ada, b_ada):
    depth, d, n = w_ada.shape
    bsz = c.shape[0]
    tn = min(d, 1024)
    assert n % tn == 0
    return pl.pallas_call(
        _ada_kernel,
        grid=(depth, n // tn),
        in_specs=[
            pl.BlockSpec((bsz, d), lambda l, j: (0, 0)),
            pl.BlockSpec((1, d, tn), lambda l, j: (l, 0, j)),
            pl.BlockSpec((1, 1, tn), lambda l, j: (l, 0, j)),
        ],
        out_specs=pl.BlockSpec((1, bsz, tn), lambda l, j: (l, 0, j)),
        out_shape=jax.ShapeDtypeStruct((depth, bsz, n), F32),
        compiler_params=_params("parallel", "parallel"),
        name="ada_proj",
    )(c, w_ada, b_ada.reshape(depth, 1, n))


def _ffn_kernel(x_ref, ada_ref, wab_ref, wo_ref, g_ref, b_ref, o_ref, h_ref, hid_ref, *, mod0, alpha):
    j = pl.program_id(2)
    n = pl.num_programs(2) - 1
    tm, d = o_ref.shape

    def up(slot):
        h = h_ref[...]
        a = _dot(h, wab_ref[0])
        b = _dot(h, wab_ref[1])
        hid_ref[slot] = (_silu(a) * b).astype(BF16)

    @pl.when(j == 0)
    def _():
        shift = ada_ref[mod0:mod0 + 1, :]
        scale = ada_ref[mod0 + 1:mod0 + 2, :]
        x = x_ref[...]
        h_ref[...] = (x * (1.0 + scale) + shift).astype(BF16)
        o_ref[...] = alpha * x
        up(0)

    coef = 0.5 * (1.0 + ada_ref[mod0 + 2:mod0 + 3, :])

    @pl.when((j > 0) & (j < n))
    def _():
        prev = hid_ref[(j - 1) % 2]
        up(j % 2)
        cw = min(d, FFN_COL_CHUNK)
        for c in range(0, d, cw):
            o_ref[:, c:c + cw] += coef[:, c:c + cw] * _dot(prev, wo_ref[:, c:c + cw])

    @pl.when(j == n)
    def _():
        rc = min(tm, FFN_ROW_CHUNK)
        for r in range(0, tm, rc):
            y = o_ref[r:r + rc, :] + coef * _dot(hid_ref[(j - 1) % 2, r:r + rc, :], wo_ref[...])
            o_ref[r:r + rc, :] = _layer_norm(y, g_ref[...], b_ref[...])


def _ffn(x, ada_l, w_ab, w_out, ln_g, ln_b, *, layer, mod0, alpha, tm, tf):
    bsz, s, d = x.shape
    n = w_out.shape[1] // tf
    kern = functools.partial(_ffn_kernel, mod0=mod0, alpha=alpha)
    return pl.pallas_call(
        kern,
        grid=(bsz, s // tm, n + 1),
        in_specs=[
            pl.BlockSpec((None, tm, d), _x_lookahead_map(bsz, s // tm, n + 1)),
            pl.BlockSpec((None, N_MOD, d), lambda b, i, j: (b, 0, 0)),
            pl.BlockSpec((None, 2, d, tf), lambda b, i, j: (layer, 0, 0, jnp.minimum(j, n - 1))),
            pl.BlockSpec((None, tf, d), lambda b, i, j: (layer, jnp.maximum(j - 1, 0), 0)),
            pl.BlockSpec((1, d), lambda b, i, j: (0, 0)),
            pl.BlockSpec((1, d), lambda b, i, j: (0, 0)),
        ],
        out_specs=pl.BlockSpec((None, tm, d), lambda b, i, j: (b, i, 0)),
        out_shape=jax.ShapeDtypeStruct((bsz, s, d), F32),
        scratch_shapes=[pltpu.VMEM((tm, d), BF16), pltpu.VMEM((2, tm, tf), BF16)],
        compiler_params=_params("parallel", "parallel", "arbitrary"),
        name="ffn",
    )(x, ada_l, w_ab, w_out, ln_g, ln_b)


def _in_proj_kernel(x_ref, ada_ref, w_ref, wc_ref, o_ref, ab_ref, h_ref, *, mod0, n_heads):
    j = pl.program_id(2)
    q_lo, q_hi, gate_lo = FOURIER_GROUPS, FOURIER_GROUPS + n_heads, FOURIER_GROUPS + 4 * n_heads

    @pl.when(j == 0)
    def _():
        shift = ada_ref[mod0:mod0 + 1, :]
        scale = ada_ref[mod0 + 1:mod0 + 2, :]
        h_ref[...] = (x_ref[...] * (1.0 + scale) + shift).astype(BF16)

    n_slabs = o_ref.shape[0]
    sub = min(n_slabs, PROJ_SUB_SLABS)
    for c0 in range(0, n_slabs, sub):
        res = _dot(h_ref[...], w_ref[:, c0 * LANES:(c0 + sub) * LANES])
        slab = j * n_slabs + c0
        is_silu = ((slab >= q_lo) & (slab < q_hi)) | (slab >= gate_lo)
        res = jnp.where(is_silu, _silu(res), res)
        for c in range(sub):
            o_ref[c0 + c] = res[:, c * LANES:(c + 1) * LANES].astype(BF16)

    @pl.when(j == 0)
    def _():
        for g in range(FOURIER_GROUPS):
            u = o_ref[g]
            cs = _dot(u, wc_ref[...])
            ab_ref[0, :, g * LANES:(g + 1) * LANES] = cs[:, :LANES].astype(BF16)
            ab_ref[1, :, g * LANES:(g + 1) * LANES] = cs[:, LANES:].astype(BF16)


def _in_proj(x, ada_l, w, w_chan, *, layer, mod0, tm, tn, n_heads):
    bsz, s, d = x.shape
    n = w.shape[2]
    assert n_heads % PROJ_SUB_SLABS == 0 and FOURIER_GROUPS == PROJ_SUB_SLABS
    kern = functools.partial(_in_proj_kernel, mod0=mod0, n_heads=n_heads)
    return pl.pallas_call(
        kern,
        grid=(bsz, s // tm, n // tn),
        in_specs=[
            pl.BlockSpec((None, tm, d), _x_lookahead_map(bsz, s // tm, n // tn)),
            pl.BlockSpec((None, N_MOD, d), lambda b, i, j: (b, 0, 0)),
            pl.BlockSpec((None, d, tn), lambda b, i, j: (layer, 0, j)),
            pl.BlockSpec((LANES, 2 * LANES), lambda b, i, j: (0, 0)),
        ],
        out_specs=[
            pl.BlockSpec((None, tn // LANES, tm, LANES), lambda b, i, j: (b, j, i, 0)),
            pl.BlockSpec((2, tm, D_FOURIER), lambda b, i, j: (0, i, b)),
        ],
        out_shape=[
            jax.ShapeDtypeStruct((bsz, n // LANES, s, LANES), BF16),
            jax.ShapeDtypeStruct((2, s, bsz * D_FOURIER), BF16),
        ],
        scratch_shapes=[pltpu.VMEM((tm, d), BF16)],
        compiler_params=_params("parallel", "parallel", "arbitrary"),
        name="in_proj",
    )(x, ada_l, w, w_chan)


def _seq_dft_kernel(m_ref, ab_ref, g_ref, o_ref, acc_ref, *, scale, nb):
    k = pl.program_id(2)

    @pl.when(k == 0)
    def _():
        acc_ref[...] = jnp.zeros_like(acc_ref)

    acc_ref[...] += _dot(m_ref[...], ab_ref[...])

    @pl.when(k == pl.num_programs(2) - 1)
    def _():
        for bb in range(nb):
            for g in range(FOURIER_GROUPS):
                c0 = (bb * FOURIER_GROUPS + g) * LANES
                y = acc_ref[:, c0:c0 + LANES] * scale
                y = y * lax.rsqrt(jnp.mean(y * y, axis=-1, keepdims=True) + RMS_EPS)
                o_ref[bb, g] = (y * g_ref[:, g * LANES:(g + 1) * LANES]).astype(BF16)


def _seq_dft(dft_mat, ab, gain, *, bsz, scale, tm, nb, tk):
    s = dft_mat.shape[0]
    k_total = dft_mat.shape[1]
    tn = nb * D_FOURIER
    kern = functools.partial(_seq_dft_kernel, scale=scale, nb=nb)
    return pl.pallas_call(
        kern,
        grid=(bsz // nb, s // tm, k_total // tk),
        in_specs=[
            pl.BlockSpec((tm, tk), lambda n, i, k: (i, k)),
            pl.BlockSpec((tk, tn), lambda n, i, k: (k, n)),
            pl.BlockSpec((1, D_FOURIER), lambda n, i, k: (0, 0)),
        ],
        out_specs=pl.BlockSpec((nb, FOURIER_GROUPS, tm, LANES), lambda n, i, k: (n, 0, i, 0)),
        out_shape=jax.ShapeDtypeStruct((bsz, FOURIER_GROUPS, s, LANES), BF16),
        scratch_shapes=[pltpu.VMEM((tm, tn), F32)],
        compiler_params=_params("parallel", "parallel", "arbitrary"),
        name="seq_dft",
    )(dft_mat, ab, gain)


def _tri_cumsum(tri, x):
    x16 = x.astype(BF16)
    n = tri.shape[0]
    return jnp.concatenate([_dot(tri, x16[r:r + n, :]) for r in range(0, x.shape[0], n)], axis=0)


def _gla_kernel(q_ref, gt_ref, v_ref, zf_ref, zb_ref, lb_ref, gain_ref, tl_ref, tu_ref, o_ref,
                p_ref, kv_ref, qst_ref, sall_ref, d_ref, *, layer, group, out_group):
    s = q_ref.shape[0]
    n_chunks = s // CHUNK
    rows_g = group * CHUNK
    n_groups = n_chunks // group

    lb_raw = lb_ref[...]
    lb_exp = jnp.exp(lb_raw - jnp.max(lb_raw, axis=0, keepdims=True))
    lb_soft = lb_exp / jnp.sum(lb_exp, axis=0, keepdims=True)
    lbs = jnp.sum(lb_soft[:layer + 1], axis=0) - lb_soft[0]
    lb_f = lbs[0:1, :]
    lb_b = lbs[1:2, :]

    row = lax.broadcasted_iota(jnp.int32, (CHUNK, CHUNK), 0)
    col = lax.broadcasted_iota(jnp.int32, (CHUNK, CHUNK), 1)
    lower = row >= col
    upper = row <= col

    def gate_terms(z_ref, rows, lb):
        sig = 1.0 / (1.0 + jnp.exp2(z_ref[rows, :].astype(F32) * (-LOG2_E)))
        f = lb + (1.0 - lb) * sig
        return 1.0 - f, jnp.log(f)

    def scaled(q3, k3, b3, i_ref, i_last):
        b_ref = b3[:, i_ref:i_ref + 1, :]
        b_last = b3[:, i_last:i_last + 1, :]
        q_in = q3 * jnp.exp2(b3 - b_ref)
        k_in = k3 * jnp.exp2(b_ref - b3)
        q_st = q_in * jnp.exp2(b_ref)
        k_st = k_in * jnp.exp2(b_last - b_ref)
        return q_in, k_in, q_st, k_st, jnp.exp2(b_last)

    def pass0(gi, carry):
        r0 = pl.multiple_of(gi * rows_g, rows_g)
        rows = pl.ds(r0, rows_g)
        q3 = q_ref[rows, :].astype(F32).reshape(group, CHUNK, LANES)
        v = v_ref[rows, :]
        k_f, logf_f = gate_terms(zf_ref, rows, lb_f)
        k_b, logf_b = gate_terms(zb_ref, rows, lb_b)
        b_f = _tri_cumsum(tl_ref[...], logf_f) * LOG2_E
        b_b = _tri_cumsum(tu_ref[...], logf_b) * LOG2_E
        shape3 = (group, CHUNK, LANES)
        qf_in, kf_in, qf_st, kf_st, d_f = scaled(q3, k_f.reshape(shape3), b_f.reshape(shape3),
                                                 CHUNK // 2, CHUNK - 1)
        qb_in, kb_in, qb_st, kb_st, d_b = scaled(q3, k_b.reshape(shape3), b_b.reshape(shape3),
                                                 CHUNK - 1 - CHUNK // 2, 0)
        qst = jnp.concatenate([qf_st.reshape(rows_g, LANES), qb_st.reshape(rows_g, LANES)], axis=1)
        qst_ref[rows, :] = qst.astype(BF16)
        for g in range(group):
            c = gi * group + g
            s_f = _dot_nt(qf_in[g].astype(BF16), kf_in[g].astype(BF16))
            s_b = _dot_nt(qb_in[g].astype(BF16), kb_in[g].astype(BF16))
            p_ref[c] = (jnp.where(lower, s_f, 0.0) + jnp.where(upper, s_b, 0.0)).astype(BF16)
            k_st = jnp.concatenate([kf_st[g], kb_st[g]], axis=1).astype(BF16)
            v_t = v[g * CHUNK:(g + 1) * CHUNK, :].T
            kv_ref[c] = _dot(v_t, k_st)
            d_ref[c] = jnp.concatenate([d_f[g], d_b[g]], axis=1)
        return carry

    lax.fori_loop(0, n_groups, pass0, 0)

    def pass1(i, carry):
        st_f, st_b = carry
        cf = i
        cb = n_chunks - 1 - i
        sall_ref[cf, :, 0:LANES] = st_f.astype(BF16)
        sall_ref[cb, :, LANES:2 * LANES] = st_b.astype(BF16)
        st_f = d_ref[cf, :, 0:LANES] * st_f + kv_ref[cf, :, 0:LANES]
        st_b = d_ref[cb, :, LANES:2 * LANES] * st_b + kv_ref[cb, :, LANES:2 * LANES]
        return st_f, st_b

    zero = jnp.zeros((LANES, LANES), F32)
    lax.fori_loop(0, n_chunks, pass1, (zero, zero), unroll=2)

    gain = gain_ref[...]

    rows_o = out_group * CHUNK

    def pass2(gi, carry):
        r0 = pl.multiple_of(gi * rows_o, rows_o)
        rows = pl.ds(r0, rows_o)
        outs = []
        for g in range(out_group):
            c = gi * out_group + g
            rc = pl.ds(pl.multiple_of(r0 + g * CHUNK, CHUNK), CHUNK)
            o = _dot(p_ref[c], v_ref[rc, :]) + _dot_nt(qst_ref[rc, :], sall_ref[c])
            outs.append(o)
        o = jnp.concatenate(outs, axis=0)
        y = o * lax.rsqrt(jnp.mean(o * o, axis=-1, keepdims=True) + RMS_EPS) * gain
        o_ref[rows, :] = (y * gt_ref[rows, :].astype(F32)).astype(BF16)
        return carry

    lax.fori_loop(0, n_chunks // out_group, pass2, 0)


def _block_tri(n, upper):
    r = lax.broadcasted_iota(jnp.int32, (n, n), 0)
    c = lax.broadcasted_iota(jnp.int32, (n, n), 1)
    same = (r // CHUNK) == (c // CHUNK)
    tri = (r <= c) if upper else (r >= c)
    return (same & tri).astype(BF16)


def _gla(proj, lower_bounds, hgrn_g_l, *, layer, n_heads, group):
    bsz, _, s, _ = proj.shape
    depth = lower_bounds.shape[0]
    n_chunks = s // CHUNK
    rows_t = min(group * CHUNK, CUMSUM_ROWS)

    def slab(j):
        return pl.BlockSpec((None, None, s, LANES), lambda b, h: (b, FOURIER_GROUPS + j * n_heads + h, 0, 0))

    out_group = GLA_OUT_GROUP if n_chunks % GLA_OUT_GROUP == 0 else group
    kern = functools.partial(_gla_kernel, layer=layer, group=group, out_group=out_group)
    return pl.pallas_call(
        kern,
        grid=(bsz, n_heads),
        in_specs=[
            slab(0), slab(4), slab(1), slab(2), slab(3),
            pl.BlockSpec((depth, 2, LANES), lambda b, h: (0, 0, h)),
            pl.BlockSpec((1, LANES), lambda b, h: (0, h)),
            pl.BlockSpec((rows_t, rows_t), lambda b, h: (0, 0)),
            pl.BlockSpec((rows_t, rows_t), lambda b, h: (0, 0)),
        ],
        out_specs=pl.BlockSpec((None, None, s, LANES), lambda b, h: (b, h, 0, 0)),
        out_shape=jax.ShapeDtypeStruct((bsz, n_heads, s, LANES), BF16),
        scratch_shapes=[
            pltpu.VMEM((n_chunks, CHUNK, CHUNK), BF16),
            pltpu.VMEM((n_chunks, LANES, 2 * LANES), F32),
            pltpu.VMEM((s, 2 * LANES), BF16),
            pltpu.VMEM((n_chunks, LANES, 2 * LANES), BF16),
            pltpu.VMEM((n_chunks, 1, 2 * LANES), F32),
        ],
        compiler_params=_params("parallel", "parallel"),
        name="gla",
    )(proj, proj, proj, proj, proj, lower_bounds, hgrn_g_l,
      _block_tri(rows_t, False), _block_tri(rows_t, True))


def _out_proj_kernel(yf_ref, yh_ref, x_ref, ada_ref, w_ref, g_ref, b_ref, o_ref, *, mod0, alpha):
    coef = 1.0 + ada_ref[mod0 + 2:mod0 + 3, :]
    tm = o_ref.shape[0]
    rc = min(tm, FFN_ROW_CHUNK)
    for r in range(0, tm, rc):
        parts = ([yf_ref[g, r:r + rc, :] for g in range(yf_ref.shape[0])]
                 + [yh_ref[h, r:r + rc, :] for h in range(yh_ref.shape[0])])
        y = _dot(jnp.concatenate(parts, axis=-1), w_ref[...])
        o_ref[r:r + rc, :] = _layer_norm(alpha * x_ref[r:r + rc, :] + coef * y, g_ref[...], b_ref[...])


def _out_proj(yf, yh, x, ada_l, w, ln_g, ln_b, *, layer, mod0, alpha, tm):
    bsz, s, d = x.shape
    n_heads = yh.shape[1]
    kern = functools.partial(_out_proj_kernel, mod0=mod0, alpha=alpha)
    return pl.pallas_call(
        kern,
        grid=(bsz, s // tm),
        in_specs=[
            pl.BlockSpec((None, FOURIER_GROUPS, tm, LANES), lambda b, i: (b, 0, i, 0)),
            pl.BlockSpec((None, n_heads, tm, LANES), lambda b, i: (b, 0, i, 0)),
            pl.BlockSpec((None, tm, d), lambda b, i: (b, i, 0)),
            pl.BlockSpec((None, N_MOD, d), lambda b, i: (b, 0, 0)),
            pl.BlockSpec((None,) + w.shape[1:], lambda b, i: (layer, 0, 0)),
            pl.BlockSpec((1, d), lambda b, i: (0, 0)),
            pl.BlockSpec((1, d), lambda b, i: (0, 0)),
        ],
        out_specs=pl.BlockSpec((None, tm, d), lambda b, i: (b, i, 0)),
        out_shape=jax.ShapeDtypeStruct((bsz, s, d), F32),
        compiler_params=_params("parallel", "parallel"),
        name="out_proj",
    )(yf, yh, x, ada_l, w, ln_g, ln_b)


def _dft_tables(s, n_cols, stride=1):
    k = lax.broadcasted_iota(jnp.int32, (s, n_cols), 0)
    n = lax.broadcasted_iota(jnp.int32, (s, n_cols), 1)
    ang = ((k * (n * stride)) % s).astype(F32) * (2.0 * jnp.pi / s)
    return jnp.cos(ang), jnp.sin(ang)


def _seq_dft_matrix_kernel(ac_ref, as_ref, bc_ref, bs_ref, o_ref):
    s = o_ref.shape[1] // 2
    b_c = bc_ref[...]
    b_s = bs_ref[...]
    for n1 in range(s // LANES):
        a_c = ac_ref[:, n1:n1 + 1]
        a_s = as_ref[:, n1:n1 + 1]
        o_ref[:, n1 * LANES:(n1 + 1) * LANES] = (a_c * b_c - a_s * b_s).astype(BF16)
        o_ref[:, s + n1 * LANES:s + (n1 + 1) * LANES] = (a_s * b_c + a_c * b_s).astype(BF16)


def _seq_dft_matrix(s):
    n_coarse = s // LANES
    a_c, a_s = _dft_tables(s, n_coarse, stride=LANES)
    b_c, b_s = _dft_tables(s, LANES)
    tm = min(s, PREP_ROWS)
    coarse = pl.BlockSpec((tm, n_coarse), lambda i: (i, 0))
    fine = pl.BlockSpec((tm, LANES), lambda i: (i, 0))
    return pl.pallas_call(
        _seq_dft_matrix_kernel,
        grid=(s // tm,),
        in_specs=[coarse, coarse, fine, fine],
        out_specs=pl.BlockSpec((tm, 2 * s), lambda i: (i, 0)),
        out_shape=jax.ShapeDtypeStruct((s, 2 * s), BF16),
        compiler_params=_params("parallel"),
        name="seq_dft_matrix",
    )(a_c, a_s, b_c, b_s)


def _cast_pad_cols_kernel(w_ref, o_ref):
    n = w_ref.shape[-1]
    o_ref[:, :n] = w_ref[...].astype(BF16)
    if o_ref.shape[-1] > n:
        o_ref[:, n:] = jnp.zeros((o_ref.shape[0], o_ref.shape[-1] - n), BF16)


def _cast_pad_rows_kernel(w_ref, o_ref):
    n = w_ref.shape[0]
    o_ref[:n, :] = w_ref[...].astype(BF16)
    if o_ref.shape[0] > n:
        o_ref[n:, :] = jnp.zeros((o_ref.shape[0] - n, o_ref.shape[1]), BF16)


def _prep_ffn_weights(w_in, w_out):
    depth, d, two_ff = w_in.shape
    d_ff = two_ff // 2
    fp = -(-d_ff // FF_ALIGN) * FF_ALIGN
    assert d_ff % LANES == 0
    tr = min(d, PREP_ROWS)
    w_ab = pl.pallas_call(
        _cast_pad_cols_kernel,
        grid=(depth, 2, d // tr),
        in_specs=[pl.BlockSpec((None, tr, d_ff), lambda l, h, i: (l, i, h))],
        out_specs=pl.BlockSpec((None, None, tr, fp), lambda l, h, i: (l, h, i, 0)),
        out_shape=jax.ShapeDtypeStruct((depth, 2, d, fp), BF16),
        compiler_params=_params("parallel", "parallel", "parallel"),
        name="prep_w_in",
    )(w_in)
    tc = min(d, PREP_ROWS)
    w_o = pl.pallas_call(
        _cast_pad_rows_kernel,
        grid=(depth, d // tc),
        in_specs=[pl.BlockSpec((None, d_ff, tc), lambda l, i: (l, 0, i))],
        out_specs=pl.BlockSpec((None, fp, tc), lambda l, i: (l, 0, i)),
        out_shape=jax.ShapeDtypeStruct((depth, fp, d), BF16),
        compiler_params=_params("parallel", "parallel"),
        name="prep_w_out",
    )(w_out)
    return w_ab, w_o


def kernel(x, c, w_ada, b_ada, w_ffn1_in, w_ffn1_out, ln1_g, ln1_b, w_in, lower_bounds, fourier_g, hgrn_g,
           w_out, ln2_g, ln2_b, w_ffn2_in, w_ffn2_out, ln3_g, ln3_b):
    bsz, s, d = x.shape
    depth = w_ada.shape[0]
    d_hgrn = hgrn_g.shape[1]
    d_in = w_in.shape[2]
    n_heads = d_hgrn // LANES
    assert fourier_g.shape[1] == D_FOURIER and d_in == D_FOURIER + 5 * d_hgrn
    alpha = float((2 * depth) ** 0.25)

    tm = min(s, 1024)
    tm_out = min(s, 512)
    tn_proj = next(t for t in (2048, 1024, 512) if d_in % t == 0)
    assert (2 * d_hgrn) % (PROJ_SUB_SLABS * LANES) == 0 and D_FOURIER == PROJ_SUB_SLABS * LANES
    assert s % (GLA_GROUP * CHUNK) == 0
    nb = 2 if bsz % 2 == 0 else 1
    tk_dft = min(2 * s, 2048)

    ada = _ada_proj(c, w_ada, b_ada).reshape(depth, bsz, N_MOD, d)

    dft_mat = _seq_dft_matrix(s)
    cos_c, sin_c = _dft_tables(LANES, LANES)
    w_chan = jnp.concatenate([cos_c, -sin_c], axis=1).astype(BF16)
    dft_scale = float((s * LANES) ** -0.5)

    w_ab1, w_o1 = _prep_ffn_weights(w_ffn1_in, w_ffn1_out)
    w_ab2, w_o2 = _prep_ffn_weights(w_ffn2_in, w_ffn2_out)
    w_mix_in = w_in.astype(BF16)
    w_mix_out = w_out.astype(BF16)

    for l in range(depth):
        ada_l = ada[l]
        x = _ffn(x, ada_l, w_ab1, w_o1, ln1_g[l][None], ln1_b[l][None],
                 layer=l, mod0=0, alpha=alpha, tm=tm, tf=FF_ALIGN)

        proj, ab = _in_proj(x, ada_l, w_mix_in, w_chan,
                            layer=l, mod0=3, tm=tm, tn=tn_proj, n_heads=n_heads)
        yf = _seq_dft(dft_mat, ab.reshape(2 * s, bsz * D_FOURIER), fourier_g[l][None],
                      bsz=bsz, scale=dft_scale, tm=tm, nb=nb, tk=tk_dft)
        yh = _gla(proj, lower_bounds, hgrn_g[l][None], layer=l, n_heads=n_heads, group=GLA_GROUP)
        x = _out_proj(yf, yh, x, ada_l, w_mix_out, ln2_g[l][None], ln2_b[l][None],
                      layer=l, mod0=3, alpha=alpha, tm=tm_out)

        x = _ffn(x, ada_l, w_ab2, w_o2, ln3_g[l][None], ln3_b[l][None],
                 layer=l, mod0=6, alpha=alpha, tm=tm, tf=FF_ALIGN)
    return x
```
